```python
import math
import jax, jax.numpy as jnp
from jax import lax
import numpy as np

D_MODEL = 1024
BATCH = 32
SEQ = 2048
DEPTH = 1
DEC_BATCH = 16
DEC_SEQ = 4096
PAST_LEN = 128

PLE_DIM = 256
N_ATTN_HEADS = 8
ATTN_HEAD_DIM = 64
ATTN_WIDTH = N_ATTN_HEADS * ATTN_HEAD_DIM
DILATED_PATTERNS = ((128, 1), (512, 4), (2048, 16))
REL_BUCKETS = 32
REL_MAX_DISTANCE = 1024
SSD_HEADS = 8
SSD_HEAD_DIM = 64
SSD_INNER = SSD_HEADS * SSD_HEAD_DIM
SSD_GROUPS = 2
SSD_STATE = 128
SSD_CONV = 5
SSD_CHUNK = 128
CONV_DIM = SSD_INNER + 2 * SSD_GROUPS * SSD_STATE
MIX_WIDTH = ATTN_WIDTH + SSD_INNER
IN_PROJ = 3 * ATTN_WIDTH + SSD_INNER + CONV_DIM + 2 * SSD_HEADS
PROJ_SPLITS = (ATTN_WIDTH, 2 * ATTN_WIDTH, 3 * ATTN_WIDTH, 3 * ATTN_WIDTH + SSD_INNER, 3 * ATTN_WIDTH + SSD_INNER + CONV_DIM)
N_EXPERTS = 16
EXPERT_FF = 2816
CAPACITY_FACTOR = 2
EPS = 1e-6

kernel_name = 'hybrid_dilated_ssd_expert_choice_encoder'


def _rmsnorm(x, g):
    xf = x.astype(jnp.float32)
    y = xf * lax.rsqrt(jnp.mean(xf * xf, axis=-1, keepdims=True) + EPS)
    return (y * g.astype(jnp.float32)).astype(x.dtype)


def _t5_bucket(rel):
    half = REL_BUCKETS // 2
    max_exact = half // 2
    n = np.abs(rel)
    large = max_exact + (np.log(np.maximum(n, 1) / max_exact) / math.log(REL_MAX_DISTANCE / max_exact) * (half - max_exact)).astype(np.int32)
    large = np.minimum(large, half - 1)
    return (np.where(rel > 0, half, 0) + np.where(n < max_exact, n, large)).astype(np.int32)


def _dilated_band_attention(q, k, v, rel_bias, window, dilation):
    b, s, h, e = q.shape
    half = window // (2 * dilation)
    blk = half
    n = s // dilation
    nb = -(-n // blk)
    npad = nb * blk

    def to_sub(a):
        a = a.reshape(b, n, dilation, h, e).transpose(0, 2, 1, 3, 4)
        return jnp.pad(a, ((0, 0), (0, 0), (0, npad - n), (0, 0), (0, 0)))

    def band(a):
        a = jnp.pad(to_sub(a), ((0, 0), (0, 0), (blk, blk), (0, 0), (0, 0))).reshape(b, dilation, nb + 2, blk, h, e)
        return jnp.concatenate([a[:, :, :-2], a[:, :, 1:-1], a[:, :, 2:]], axis=3)

    qb = to_sub(q).reshape(b, dilation, nb, blk, h, e)
    kb, vb = band(k), band(v)
    qi = np.arange(blk)[:, None]
    ki = np.arange(3 * blk)[None, :]
    off = ki - blk - qi
    u = np.arange(nb)[:, None, None] * blk - blk + ki[None]
    mask = (np.abs(off) <= half)[None] & (u >= 0) & (u < n)
    bias = rel_bias[_t5_bucket(off * dilation)].astype(jnp.float32).transpose(2, 0, 1)
    logits = jnp.einsum('bdnqhe,bdnkhe->bdnhqk', qb, kb) * (ATTN_HEAD_DIM ** -0.5) + bias
    logits = jnp.where(mask[:, None], logits, -jnp.inf)
    m = jnp.max(logits, axis=-1, keepdims=True)
    pexp = jnp.exp(logits - m)
    ssum = jnp.sum(pexp, axis=-1, keepdims=True)
    out = jnp.einsum('bdnhqk,bdnkhe->bdnqhe', pexp, vb) / ssum[..., 0].transpose(0, 1, 2, 4, 3)[..., None]
    lse = (m + jnp.log(ssum))[..., 0].transpose(0, 1, 2, 4, 3)
    out = out.reshape(b, dilation, npad, h, e)[:, :, :n].transpose(0, 2, 1, 3, 4).reshape(b, s, h, e)
    lse = lse.reshape(b, dilation, npad, h)[:, :, :n].transpose(0, 2, 1, 3).reshape(b, s, h)
    return out, lse


def _dilated_mixture(q, k, v, rel_bias):
    q, k, v = (a.astype(jnp.float32) for a in (q, k, v))
    outs, lses = [], []
    for window, dilation in DILATED_PATTERNS:
        o, l = _dilated_band_attention(q, k, v, rel_bias, window, dilation)
        outs.append(o)
        lses.append(l)
    w = jax.nn.softmax(jnp.stack(lses), axis=0)
    return jnp.sum(w[..., None] * jnp.stack(outs), axis=0)


def _ssd_scan(x, dt, A, bm, cm):
    b, s, h, p = x.shape
    g, n = bm.shape[2], bm.shape[3]
    r = h // g
    l = SSD_CHUNK
    c = s // l
    xdt = (x * dt[..., None]).reshape(b, c, l, g, r, p)
    a = (dt * A).reshape(b, c, l, g, r).transpose(0, 3, 4, 1, 2)
    a_cum = jnp.cumsum(a, axis=-1)
    bc = bm.reshape(b, c, l, g, n)
    cc = cm.reshape(b, c, l, g, n)
    seg = a_cum[..., :, None] - a_cum[..., None, :]
    tril = np.tril(np.ones((l, l), dtype=bool))
    decay = jnp.exp(jnp.where(tril, seg, -jnp.inf))
    cb = jnp.einsum('bclgn,bcsgn->bgcls', cc, bc)
    y_diag = jnp.einsum('bgcls,bgrcls,bcsgrp->bclgrp', cb, decay, xdt)
    decay_states = jnp.exp(a_cum[..., -1:] - a_cum)
    states = jnp.einsum('bclgn,bgrcl,bclgrp->bcgrpn', bc, decay_states, xdt)
    chunk_decay = jnp.exp(a_cum[..., -1])

    def step(hstate, inp):
        st, dec = inp
        return hstate * dec[..., None, None] + st, hstate

    h0 = jnp.zeros((b, g, r, p, n), jnp.float32)
    _, h_in = lax.scan(step, h0, (states.transpose(1, 0, 2, 3, 4, 5), chunk_decay.transpose(3, 0, 1, 2)))
    y_off = jnp.einsum('bclgn,cbgrpn,bgrcl->bclgrp', cc, h_in, jnp.exp(a_cum))
    return (y_diag + y_off).reshape(b, s, h, p)


def _ssd_mixer(z, xbc, dt_raw, conv_w, conv_b, dt_bias, a_log, d_skip, g_norm):
    b, s, _ = z.shape
    xbc = lax.conv_general_dilated(xbc, conv_w[:, None, :].astype(xbc.dtype), (1,), [(SSD_CONV // 2, SSD_CONV // 2)],
                                   dimension_numbers=('NWC', 'WIO', 'NWC'), feature_group_count=CONV_DIM)
    xbc = jax.nn.silu(xbc + conv_b.astype(xbc.dtype)).astype(jnp.float32)
    xs, bm, cm = jnp.split(xbc, [SSD_INNER, SSD_INNER + SSD_GROUPS * SSD_STATE], axis=-1)
    xs = xs.reshape(b, s, SSD_HEADS, SSD_HEAD_DIM)
    bm = bm.reshape(b, s, SSD_GROUPS, SSD_STATE)
    cm = cm.reshape(b, s, SSD_GROUPS, SSD_STATE)
    dt = jax.nn.softplus(dt_raw.astype(jnp.float32).reshape(b, s, 2, SSD_HEADS) + dt_bias.astype(jnp.float32))
    A = -jnp.exp(a_log.astype(jnp.float32))
    y_fwd = _ssd_scan(xs, dt[:, :, 0], A[0], bm, cm)
    flip = lambda a: a[:, ::-1]
    y_bwd = flip(_ssd_scan(flip(xs), flip(dt[:, :, 1]), A[1], flip(bm), flip(cm)))
    y = y_fwd + y_bwd + d_skip.astype(jnp.float32)[:, None] * xs
    y = y.reshape(b, s, SSD_INNER) * jax.nn.silu(z.astype(jnp.float32))
    yg = y.reshape(b, s, SSD_GROUPS, SSD_INNER // SSD_GROUPS)
    yg = yg * lax.rsqrt(jnp.mean(yg * yg, axis=-1, keepdims=True) + EPS)
    return (yg.reshape(b, s, SSD_INNER) * g_norm.astype(jnp.float32)).astype(z.dtype)


def _expert_choice_ffn(h, w_router, w_gate, w_up, w_down):
    t = h.shape[0]
    cap = max(1, CAPACITY_FACTOR * t // N_EXPERTS)
    aff = jax.nn.softmax((h @ w_router).astype(jnp.float32), axis=-1)
    gate, idx = lax.top_k(aff.T, cap)
    xs = h[idx]

    def expert(args):
        xe, wg, wu, wd = args
        return (jax.nn.silu(xe @ wg) * (xe @ wu)) @ wd

    ye = lax.map(expert, (xs, w_gate, w_up, w_down)) * gate[..., None].astype(h.dtype)
    return jnp.zeros_like(h).at[idx.reshape(-1)].add(ye.reshape(-1, h.shape[1]))


def _layer(x, pe, rel_bias, g_mix, w_in, conv_w, conv_b, dt_bias, a_log, d_skip, g_ssd, w_out,
           g_ffn, w_router, w_gate, w_up, w_down, g_pg, w_pg, w_ple, g_ple):
    b, s, _ = x.shape
    h = _rmsnorm(x, g_mix)
    proj = h @ w_in
    q, k, v, z, xbc, dt_raw = jnp.split(proj, PROJ_SPLITS, axis=-1)
    heads = lambda a: a.reshape(b, s, N_ATTN_HEADS, ATTN_HEAD_DIM)
    attn = _dilated_mixture(heads(q), heads(k), heads(v), rel_bias).reshape(b, s, ATTN_WIDTH).astype(x.dtype)
    ssd = _ssd_mixer(z, xbc, dt_raw, conv_w, conv_b, dt_bias, a_log, d_skip, g_ssd)
    x = x + jnp.concatenate([attn, ssd], axis=-1) @ w_out
    hn = _rmsnorm(x, g_ffn)
    x = x + _expert_choice_ffn(hn.reshape(b * s, D_MODEL), w_router, w_gate, w_up, w_down).reshape(b, s, D_MODEL)
    e = _rmsnorm(pe @ w_ple, g_ple)
    gate = jax.nn.sigmoid(_rmsnorm(x, g_pg) @ w_pg)
    return x + gate * e


def _trunk(x, p, rel_bias, g_mix, w_in, conv_w, conv_b, dt_bias, a_log, d_skip, g_ssd, w_out,
           g_ffn, w_router, w_gate, w_up, w_down, g_pg, w_pg, w_ple, g_ple, g_final):
    for i in range(DEPTH):
        x = _layer(x, p[i], rel_bias, g_mix[i], w_in[i], conv_w[i], conv_b[i], dt_bias[i], a_log[i], d_skip[i],
                   g_ssd[i], w_out[i], g_ffn[i], w_router[i], w_gate[i], w_up[i], w_down[i], g_pg[i], w_pg[i],
                   w_ple[i], g_ple[i])
    return _rmsnorm(x, g_final)


def setup_inputs(seed: int = 0) -> dict:
    key = jax.random.key(seed)
    ks = jax.random.split(key, 24)
    f32 = jnp.float32
    nrm = lambda k, shape, scale: jax.random.normal(k, shape, f32) * scale
    gain = lambda k, shape: 1.0 + 0.05 * jax.random.normal(k, shape, f32)
    dt0 = jnp.exp(jax.random.uniform(ks[10], (DEPTH, 2, SSD_HEADS), f32, math.log(1e-3), math.log(1e-1)))
    return {
        'x_prompt': nrm(ks[0], (BATCH, SEQ, D_MODEL), 1.0),
        'x_sample': nrm(ks[1], (DEC_BATCH, DEC_SEQ, D_MODEL), 1.0),
        'p_prompt': nrm(ks[2], (DEPTH, BATCH, SEQ, PLE_DIM), 1.0),
        'p_sample': nrm(ks[3], (DEPTH, DEC_BATCH, DEC_SEQ, PLE_DIM), 1.0),
        'rel_bias': nrm(ks[4], (REL_BUCKETS, N_ATTN_HEADS), 0.2),
        'g_mix': gain(ks[5], (DEPTH, D_MODEL)),
        'w_in': nrm(ks[6], (DEPTH, D_MODEL, IN_PROJ), D_MODEL ** -0.5),
        'conv_w': nrm(ks[7], (DEPTH, SSD_CONV, CONV_DIM), SSD_CONV ** -0.5),
        'conv_b': nrm(ks[8], (DEPTH, CONV_DIM), 0.02),
        'dt_bias': dt0 + jnp.log(-jnp.expm1(-dt0)),
        'a_log': jnp.log(jax.random.uniform(ks[9], (DEPTH, 2, SSD_HEADS), f32, 1.0, 16.0)),
        'd_skip': gain(ks[11], (DEPTH, SSD_HEADS)),
        'g_ssd': gain(ks[12], (DEPTH, SSD_INNER)),
        'w_out': nrm(ks[13], (DEPTH, MIX_WIDTH, D_MODEL), MIX_WIDTH ** -0.5),
        'g_ffn': gain(ks[14], (DEPTH, D_MODEL)),
        'w_router': nrm(ks[15], (DEPTH, D_MODEL, N_EXPERTS), D_MODEL ** -0.5),
        'w_gate': nrm(ks[16], (DEPTH, N_EXPERTS, D_MODEL, EXPERT_FF), D_MODEL ** -0.5),
        'w_up': nrm(ks[17], (DEPTH, N_EXPERTS, D_MODEL, EXPERT_FF), D_MODEL ** -0.5),
        'w_down': nrm(ks[18], (DEPTH, N_EXPERTS, EXPERT_FF, D_MODEL), EXPERT_FF ** -0.5),
        'g_pg': gain(ks[19], (DEPTH, D_MODEL)),
        'w_pg': nrm(ks[20], (DEPTH, D_MODEL, D_MODEL), D_MODEL ** -0.5),
        'w_ple': nrm(ks[21], (DEPTH, PLE_DIM, D_MODEL), PLE_DIM ** -0.5),
        'g_ple': gain(ks[22], (DEPTH, D_MODEL)),
        'g_final': gain(ks[23], (D_MODEL,)),
    }


def reference(x_prompt, x_sample, p_prompt, p_sample, rel_bias, g_mix, w_in, conv_w, conv_b, dt_bias, a_log,
              d_skip, g_ssd, w_out, g_ffn, w_router, w_gate, w_up, w_down, g_pg, w_pg, w_ple, g_ple, g_final):
    y_prompt = _trunk(x_prompt, p_prompt, rel_bias, g_mix, w_in, conv_w, conv_b, dt_bias, a_log, d_skip, g_ssd,
                      w_out, g_ffn, w_router, w_gate, w_up, w_down, g_pg, w_pg, w_ple, g_ple, g_final)
    y_sample = _trunk(x_sample, p_sample, rel_bias, g_mix, w_in, conv_w, conv_b, dt_bias, a_log, d_skip, g_ssd,
                      w_out, g_ffn, w_router, w_gate, w_up, w_down, g_pg, w_pg, w_ple, g_ple, g_final)
    return (y_prompt, y_sample)
```

```python
import functools
import math

import jax
import jax.numpy as jnp
import numpy as np
from jax import lax
from jax.experimental import pallas as pl
from jax.experimental.pallas import tpu as pltpu

F32 = jnp.float32
BF16 = jnp.bfloat16
I32 = jnp.int32

D_MODEL = 1024
PLE_DIM = 256
N_ATTN_HEADS = 8
ATTN_HEAD_DIM = 64
ATTN_WIDTH = N_ATTN_HEADS * ATTN_HEAD_DIM
DILATED_PATTERNS = ((128, 1), (512, 4), (2048, 16))
REL_BUCKETS = 32
REL_MAX_DISTANCE = 1024
SSD_HEADS = 8
SSD_HEAD_DIM = 64
SSD_INNER = SSD_HEADS * SSD_HEAD_DIM
SSD_GROUPS = 2
SSD_STATE = 128
SSD_CONV = 5
SSD_CHUNK = 128
CONV_DIM = SSD_INNER + 2 * SSD_GROUPS * SSD_STATE
N_EXPERTS = 16
EXPERT_FF = 2816
CAPACITY_FACTOR = 2
EPS = 1e-6

LANES = 128
SUBLANES = 8
BF16_ROWS = 16
VMEM_LIMIT = 56 * 1024 * 1024
NEG = -1e30

HEADS_PER_GROUP = SSD_HEADS // SSD_GROUPS
GROUP_COLS = SSD_INNER // SSD_GROUPS
ATT_HALF = 64
ATT_BQ = 128
ATT_KW = 256
TM_PROJ = 512
TB_DISPATCH = 1024
TB_COMBINE = 512
STRIP = 256
TM_FFN = 512
SEL_CHUNK = 1024


def _cparams(sem):
    return pltpu.CompilerParams(dimension_semantics=sem, vmem_limit_bytes=VMEM_LIMIT)


def _rms(x, g):
    return x * lax.rsqrt(jnp.mean(x * x, axis=-1, keepdims=True) + EPS) * g


def _sigmoid(x):
    return 1.0 / (1.0 + jnp.exp(-x))


def _silu(x):
    return x * _sigmoid(x)


def _in_proj_kernel(x_ref, g_ref, wqkv_ref, wz_ref, wxbc_ref, wdt_ref, qkv_ref, z_ref, xbc_ref, dt_ref):
    h = _rms(x_ref[...], g_ref[...]).astype(BF16)
    qkv_ref[...] = jnp.dot(h, wqkv_ref[...], preferred_element_type=F32)
    z_ref[...] = jnp.dot(h, wz_ref[...], preferred_element_type=F32)
    xbc_ref[...] = jnp.dot(h, wxbc_ref[...], preferred_element_type=F32)
    for g in range(SSD_GROUPS):
        dt_ref[g] = jnp.dot(h, wdt_ref[g], preferred_element_type=F32)


def _in_proj(x2, g_mix, wqkv, wz, wxbc, wdt):
    t = x2.shape[0]
    tm = TM_PROJ
    full = lambda a: pl.BlockSpec(a.shape, lambda i: (0,) * a.ndim)
    return pl.pallas_call(
        _in_proj_kernel,
        grid=(t // tm,),
        in_specs=[pl.BlockSpec((tm, D_MODEL), lambda i: (i, 0)), full(g_mix), full(wqkv), full(wz), full(wxbc), full(wdt)],
        out_specs=[
            pl.BlockSpec((tm, 3 * ATTN_WIDTH), lambda i: (i, 0)),
            pl.BlockSpec((tm, SSD_INNER), lambda i: (i, 0)),
            pl.BlockSpec((tm, CONV_DIM), lambda i: (i, 0)),
            pl.BlockSpec((SSD_GROUPS, tm, 2 * HEADS_PER_GROUP), lambda i: (0, i, 0)),
        ],
        out_shape=[
            jax.ShapeDtypeStruct((t, 3 * ATTN_WIDTH), F32),
            jax.ShapeDtypeStruct((t, SSD_INNER), F32),
            jax.ShapeDtypeStruct((t, CONV_DIM), F32),
            jax.ShapeDtypeStruct((SSD_GROUPS, t, 2 * HEADS_PER_GROUP), F32),
        ],
        compiler_params=_cparams(("parallel",)),
        name="in_proj",
    )(x2, g_mix, wqkv, wz, wxbc, wdt)


def _t5_bucket(rel):
    half = REL_BUCKETS // 2
    max_exact = half // 2
    n = np.abs(rel)
    large = max_exact + (np.log(np.maximum(n, 1) / max_exact) / math.log(REL_MAX_DISTANCE / max_exact) * (half - max_exact)).astype(np.int32)
    large = np.minimum(large, half - 1)
    return (np.where(rel > 0, half, 0) + np.where(n < max_exact, n, large)).astype(np.int32)


_ATT_WINDOW_OFFSETS = (0, -ATT_HALF, -(ATT_KW - ATT_BQ))


def _attn_bias_tables(rel_bias):
    tabs = []
    for window, dilation in DILATED_PATTERNS:
        assert window // (2 * dilation) == ATT_HALF
        kinds = []
        for off in _ATT_WINDOW_OFFSETS:
            m = np.arange(ATT_KW)[None, :] + off - np.arange(ATT_BQ)[:, None]
            valid = np.abs(m) <= ATT_HALF
            bucket = _t5_bucket(np.clip(m, -ATT_HALF, ATT_HALF) * dilation)
            b = jnp.where(valid[..., None], rel_bias[bucket].astype(F32), NEG)
            kinds.append(b.transpose(2, 0, 1))
        tabs.append(jnp.stack(kinds))
    return jnp.stack(tabs)


def _attn_kernel(q_ref, k_ref, v_ref, bias_ref, o_ref, acc_ref, m_ref, l_ref, *, s):
    lane = lax.broadcasted_iota(I32, (ATT_BQ, LANES), 1)
    first_head = lane < ATTN_HEAD_DIM
    scale = ATTN_HEAD_DIM ** -0.5

    def rows(start, size, d):
        return pl.ds(start, size) if d == 1 else pl.ds(start, size, stride=d)

    def block(p, d, kind, kw, q_start, k_start):
        qb = q_ref[0, rows(q_start, ATT_BQ, d), :]
        kb = k_ref[0, rows(k_start, kw, d), :].astype(BF16)
        vb = v_ref[0, rows(k_start, kw, d), :].astype(BF16)
        outs, ms, ls = [], [], []
        for hh in range(2):
            sel = first_head if hh == 0 else jnp.logical_not(first_head)
            qm = jnp.where(sel, qb, 0.0).astype(BF16)
            sc = lax.dot_general(qm, kb, (((1,), (1,)), ((), ())), preferred_element_type=F32)
            sc = sc * scale + bias_ref[p, kind, hh, :, :kw]
            mx = jnp.max(sc, axis=-1, keepdims=True)
            pe = jnp.exp(sc - mx)
            ls.append(jnp.sum(pe, axis=-1, keepdims=True))
            ms.append(mx)
            outs.append(jnp.dot(pe.astype(BF16), vb, preferred_element_type=F32))
        o = jnp.where(first_head, outs[0], outs[1])
        mb = jnp.where(first_head, ms[0], ms[1])
        lb = jnp.where(first_head, ls[0], ls[1])
        idx = rows(q_start, ATT_BQ, d)
        if p == 0:
            acc_ref[idx, :] = o
            m_ref[idx, :] = mb
            l_ref[idx, :] = lb
        else:
            mo = m_ref[idx, :]
            mn = jnp.maximum(mo, mb)
            a0 = jnp.exp(mo - mn)
            a1 = jnp.exp(mb - mn)
            acc_ref[idx, :] = acc_ref[idx, :] * a0 + o * a1
            l_ref[idx, :] = l_ref[idx, :] * a0 + lb * a1
            m_ref[idx, :] = mn

    for p, (_, d) in enumerate(DILATED_PATTERNS):
        n = s // d
        nq = n // ATT_BQ
        kw = min(ATT_KW, n)

        def residue(r, p=p, d=d, n=n, nq=nq, kw=kw):
            block(p, d, 0, kw, r, r)
            if nq > 2:
                def interior(qi, carry):
                    q0 = qi * ATT_BQ
                    block(p, d, 1, kw, r + d * q0, r + d * (q0 - ATT_HALF))
                    return carry
                lax.fori_loop(1, nq - 1, interior, 0)
            if nq >= 2:
                block(p, d, 2, kw, r + d * (n - ATT_BQ), r + d * (n - ATT_KW))

        if d == 1:
            residue(0)
        else:
            def res_body(r, carry):
                residue(r)
                return carry
            lax.fori_loop(0, d, res_body, 0)

    def finish(i, carry):
        r0 = pl.multiple_of(i * ATT_BQ, ATT_BQ)
        o_ref[0, pl.ds(r0, ATT_BQ), :] = (acc_ref[pl.ds(r0, ATT_BQ), :] / l_ref[pl.ds(r0, ATT_BQ), :]).astype(o_ref.dtype)
        return carry
    lax.fori_loop(0, s // ATT_BQ, finish, 0)


def _attention(qkv3, bias_tabs):
    b, s, _ = qkv3.shape
    assert s % (16 * ATT_BQ) == 0
    npairs = N_ATTN_HEADS // 2
    col = lambda base: pl.BlockSpec((1, s, LANES), lambda bi, hp: (bi, 0, base + hp))
    return pl.pallas_call(
        functools.partial(_attn_kernel, s=s),
        grid=(b, npairs),
        in_specs=[col(0), col(npairs), col(2 * npairs),
                  pl.BlockSpec((len(DILATED_PATTERNS), len(_ATT_WINDOW_OFFSETS), 2, ATT_BQ, ATT_KW), lambda bi, hp: (0, 0, hp, 0, 0))],
        out_specs=pl.BlockSpec((1, s, LANES), lambda bi, hp: (bi, 0, hp)),
        out_shape=jax.ShapeDtypeStruct((b, s, ATTN_WIDTH), BF16),
        scratch_shapes=[pltpu.VMEM((s, LANES), F32)] * 3,
        compiler_params=_cparams(("parallel", "parallel")),
        name="attention",
    )(qkv3, qkv3, qkv3, bias_tabs)


def _expand_heads(v):
    rows = v.shape[0]
    head = lax.broadcasted_iota(I32, (rows, GROUP_COLS), 1) // SSD_HEAD_DIM
    out = jnp.broadcast_to(v[:, 0:1], (rows, GROUP_COLS))
    for j in range(1, HEADS_PER_GROUP):
        out = jnp.where(head == j, v[:, j:j + 1], out)
    return out


def _ssd_kernel(xs_ref, b_ref, c_ref, z_ref, dt_ref, wx_ref, wb_ref, wc_ref, bx_ref, bb_ref, bc_ref,
                dtb_ref, alog_ref, dsk_ref, gn_ref, o_ref, xc_ref, bcv_ref, ccv_ref, y_ref, h_ref, *, s):
    L = SSD_CHUNK
    nc = s // L
    row = lax.broadcasted_iota(I32, (L, L), 0)
    colm = lax.broadcasted_iota(I32, (L, L), 1)
    lower = row >= colm
    upper = row <= colm
    tril = lower.astype(F32)
    triu = upper.astype(F32)
    head_of_lane = lax.broadcasted_iota(I32, (L, GROUP_COLS), 1) // SSD_HEAD_DIM
    a_neg = -jnp.exp(alog_ref[0])
    dt_bias = dtb_ref[0]

    def conv(ref, w_ref, bias_ref, c, t0):
        main = ref[0, pl.ds(t0, L), :]
        p0 = pl.multiple_of(jnp.maximum(t0 - SUBLANES, 0), SUBLANES)
        n0 = pl.multiple_of(jnp.minimum(t0 + L, s - SUBLANES), SUBLANES)
        prev = ref[0, pl.ds(p0, SUBLANES), :] * jnp.where(c > 0, 1.0, 0.0)
        nxt = ref[0, pl.ds(n0, SUBLANES), :] * jnp.where(c < nc - 1, 1.0, 0.0)
        ext = jnp.concatenate([prev, main, nxt], axis=0)
        acc = jnp.broadcast_to(bias_ref[...], main.shape)
        for k in range(SSD_CONV):
            lo = SUBLANES - SSD_CONV // 2 + k
            acc = acc + w_ref[k:k + 1, :] * ext[lo:lo + L, :]
        return _silu(acc)

    def chunk(c, direction):
        t0 = pl.multiple_of(c * L, L)
        if direction == 0:
            xs = conv(xs_ref, wx_ref, bx_ref, c, t0)
            bm = conv(b_ref, wb_ref, bb_ref, c, t0)
            cm = conv(c_ref, wc_ref, bc_ref, c, t0)
            xc_ref[pl.ds(t0, L), :] = xs
            bcv_ref[pl.ds(t0, L), :] = bm
            ccv_ref[pl.ds(t0, L), :] = cm
        else:
            xs = xc_ref[pl.ds(t0, L), :]
            bm = bcv_ref[pl.ds(t0, L), :]
            cm = ccv_ref[pl.ds(t0, L), :]
        x = dt_ref[0, 0, pl.ds(t0, L), :] + dt_bias
        dt_all = jnp.maximum(x, 0.0) + jnp.log1p(jnp.exp(-jnp.abs(x)))
        a_all = dt_all * a_neg
        lo = direction * HEADS_PER_GROUP
        dt = dt_all[:, lo:lo + HEADS_PER_GROUP]
        a = a_all[:, lo:lo + HEADS_PER_GROUP]
        cum = jnp.dot(tril if direction == 0 else triu, a, preferred_element_type=F32, precision=lax.Precision.HIGHEST)
        total = cum[L - 1:L, :] if direction == 0 else cum[0:1, :]
        mask = lower if direction == 0 else upper
        bmb = bm.astype(BF16)
        cmb = cm.astype(BF16)
        cb = lax.dot_general(cmb, bmb, (((1,), (1,)), ((), ())), preferred_element_type=F32)
        xdt = (xs * _expand_heads(dt)).astype(BF16)
        y = jnp.zeros((L, GROUP_COLS), F32)
        for j in range(HEADS_PER_GROUP):
            colb = jnp.broadcast_to(cum[:, j:j + 1], (L, L))
            seg = colb - colb.T
            dec = jnp.exp(jnp.where(mask, seg, NEG))
            yj = jnp.dot((cb * dec).astype(BF16), xdt, preferred_element_type=F32)
            y = jnp.where(head_of_lane == j, yj, y)
        hprev = h_ref[...]
        y = y + jnp.dot(cmb, hprev.astype(BF16), preferred_element_type=F32) * _expand_heads(jnp.exp(cum))
        xw = (xs * _expand_heads(dt * jnp.exp(total - cum))).astype(BF16)
        st = lax.dot_general(bmb, xw, (((0,), (0,)), ((), ())), preferred_element_type=F32)
        h_ref[...] = hprev * _expand_heads(jnp.exp(total)) + st
        if direction == 0:
            y_ref[pl.ds(t0, L), :] = y
        else:
            y = y_ref[pl.ds(t0, L), :] + y + dsk_ref[...] * xs
            y = y * _silu(z_ref[0, pl.ds(t0, L), :])
            o_ref[0, pl.ds(t0, L), :] = _rms(y, gn_ref[...]).astype(o_ref.dtype)

    h_ref[...] = jnp.zeros_like(h_ref)

    def fwd(c, carry):
        chunk(c, 0)
        return carry
    lax.fori_loop(0, nc, fwd, 0)
    h_ref[...] = jnp.zeros_like(h_ref)

    def bwd(i, carry):
        chunk(nc - 1 - i, 1)
        return carry
    lax.fori_loop(0, nc, bwd, 0)


def _ssd(xbc3, z3, dt4, conv_w, conv_b, dtb, alog, dsk, gn):
    b, s, _ = xbc3.shape
    g_off_b = SSD_INNER // SSD_STATE
    g_off_c = g_off_b + SSD_GROUPS
    xcol = pl.BlockSpec((1, s, GROUP_COLS), lambda bi, g: (bi, 0, g))
    bcol = pl.BlockSpec((1, s, SSD_STATE), lambda bi, g: (bi, 0, g_off_b + g))
    ccol = pl.BlockSpec((1, s, SSD_STATE), lambda bi, g: (bi, 0, g_off_c + g))
    return pl.pallas_call(
        functools.partial(_ssd_kernel, s=s),
        grid=(b, SSD_GROUPS),
        in_specs=[
            xcol, bcol, ccol,
            pl.BlockSpec((1, s, GROUP_COLS), lambda bi, g: (bi, 0, g)),
            pl.BlockSpec((1, 1, s, 2 * HEADS_PER_GROUP), lambda bi, g: (g, bi, 0, 0)),
            pl.BlockSpec((SSD_CONV, GROUP_COLS), lambda bi, g: (0, g)),
            pl.BlockSpec((SSD_CONV, SSD_STATE), lambda bi, g: (0, g_off_b + g)),
            pl.BlockSpec((SSD_CONV, SSD_STATE), lambda bi, g: (0, g_off_c + g)),
            pl.BlockSpec((1, GROUP_COLS), lambda bi, g: (0, g)),
            pl.BlockSpec((1, SSD_STATE), lambda bi, g: (0, g_off_b + g)),
            pl.BlockSpec((1, SSD_STATE), lambda bi, g: (0, g_off_c + g)),
            pl.BlockSpec((1, 1, 2 * HEADS_PER_GROUP), lambda bi, g: (g, 0, 0)),
            pl.BlockSpec((1, 1, 2 * HEADS_PER_GROUP), lambda bi, g: (g, 0, 0)),
            pl.BlockSpec((1, GROUP_COLS), lambda bi, g: (0, g)),
            pl.BlockSpec((1, GROUP_COLS), lambda bi, g: (0, g)),
        ],
        out_specs=pl.BlockSpec((1, s, GROUP_COLS), lambda bi, g: (bi, 0, g)),
        out_shape=jax.ShapeDtypeStruct((b, s, SSD_INNER), BF16),
        scratch_shapes=[
            pltpu.VMEM((s, GROUP_COLS), F32), pltpu.VMEM((s, SSD_STATE), F32), pltpu.VMEM((s, SSD_STATE), F32),
            pltpu.VMEM((s, GROUP_COLS), F32), pltpu.VMEM((SSD_STATE, GROUP_COLS), F32),
        ],
        compiler_params=_cparams(("parallel", "parallel")),
        name="ssd",
    )(xbc3, xbc3, xbc3, z3, dt4, conv_w, conv_w, conv_w, conv_b, conv_b, conv_b, dtb, alog, dsk, gn)


def _out_proj_kernel(x_ref, attn_ref, ssd_ref, wa_ref, ws_ref, g_ref, wr_ref, x1_ref, hn_ref, aff_ref):
    x1 = x_ref[...] + jnp.dot(attn_ref[...], wa_ref[...], preferred_element_type=F32) \
        + jnp.dot(ssd_ref[...], ws_ref[...], preferred_element_type=F32)
    x1_ref[...] = x1
    hn = _rms(x1, g_ref[...])
    hn_ref[...] = hn.astype(BF16)
    logits = lax.dot_general(wr_ref[...], hn, (((1,), (1,)), ((), ())), preferred_element_type=F32,
                             precision=lax.Precision.HIGHEST)
    e = jnp.exp(logits - jnp.max(logits, axis=0, keepdims=True))
    aff_ref[...] = e / jnp.sum(e, axis=0, keepdims=True)


def _out_proj(x2, attn2, ssd2, wa, ws, g_ffn, wr_t):
    t = x2.shape[0]
    tm = TM_PROJ
    full = lambda a: pl.BlockSpec(a.shape, lambda i: (0,) * a.ndim)
    return pl.pallas_call(
        _out_proj_kernel,
        grid=(t // tm,),
        in_specs=[pl.BlockSpec((tm, D_MODEL), lambda i: (i, 0)),
                  pl.BlockSpec((tm, ATTN_WIDTH), lambda i: (i, 0)),
                  pl.BlockSpec((tm, SSD_INNER), lambda i: (i, 0)),
                  full(wa), full(ws), full(g_ffn), full(wr_t)],
        out_specs=[pl.BlockSpec((tm, D_MODEL), lambda i: (i, 0)),
                   pl.BlockSpec((tm, D_MODEL), lambda i: (i, 0)),
                   pl.BlockSpec((N_EXPERTS, tm), lambda i: (0, i))],
        out_shape=[jax.ShapeDtypeStruct((t, D_MODEL), F32),
                   jax.ShapeDtypeStruct((t, D_MODEL), BF16),
                   jax.ShapeDtypeStruct((N_EXPERTS, t), F32)],
        compiler_params=_cparams(("parallel",)),
        name="out_proj",
    )(x2, attn2, ssd2, wa, ws, g_ffn, wr_t)


def _select_kernel(aff_ref, mask_ref, gate_ref, bits_ref, *, t, cap):
    nchunk = t // SEL_CHUNK
    lane = lax.broadcasted_iota(I32, (N_EXPERTS, SEL_CHUNK), 1)

    def to_bits(i, carry):
        sl = pl.ds(pl.multiple_of(i * SEL_CHUNK, SEL_CHUNK), SEL_CHUNK)
        bits_ref[:, sl] = lax.bitcast_convert_type(aff_ref[:, sl], I32)
        return carry
    lax.fori_loop(0, nchunk, to_bits, 0)

    def count(pred):
        def body(i, acc):
            sl = pl.ds(pl.multiple_of(i * SEL_CHUNK, SEL_CHUNK), SEL_CHUNK)
            return acc + pred(bits_ref[:, sl], lane + i * SEL_CHUNK).astype(F32)
        acc = lax.fori_loop(0, nchunk, body, jnp.zeros((N_EXPERTS, SEL_CHUNK), F32))
        return jnp.sum(acc, axis=1, keepdims=True)

    thr = jnp.zeros((N_EXPERTS, 1), I32)
    for bit in range(30, -1, -1):
        cand = thr | (1 << bit)
        cnt = count(lambda b, idx, cand=cand: b >= cand)
        thr = jnp.where(cnt >= cap, cand, thr)
    need = cap - count(lambda b, idx: b > thr)
    bound = jnp.zeros((N_EXPERTS, 1), I32)
    for bit in range(t.bit_length() - 1, -1, -1):
        cand = bound | (1 << bit)
        cnt = count(lambda b, idx, cand=cand: (b == thr) & (idx < cand))
        bound = jnp.where(cnt <= need, cand, bound)

    def emit(i, carry):
        sl = pl.ds(pl.multiple_of(i * SEL_CHUNK, SEL_CHUNK), SEL_CHUNK)
        b = bits_ref[:, sl]
        sel = (b > thr) | ((b == thr) & ((lane + i * SEL_CHUNK) < bound))
        mask_ref[:, sl] = sel.astype(F32)
        gate_ref[:, sl] = jnp.where(sel, aff_ref[:, sl], 0.0)
        return carry
    lax.fori_loop(0, nchunk, emit, 0)


def _select(aff_t, cap):
    t = aff_t.shape[1]
    full = pl.BlockSpec((N_EXPERTS, t), lambda i: (0, 0))
    return pl.pallas_call(
        functools.partial(_select_kernel, t=t, cap=cap),
        grid=(1,),
        in_specs=[full],
        out_specs=[full, full],
        out_shape=[jax.ShapeDtypeStruct((N_EXPERTS, t), F32)] * 2,
        scratch_shapes=[pltpu.VMEM((N_EXPERTS, t), I32)],
        compiler_params=_cparams(("arbitrary",)),
        name="select",
    )(aff_t)


def _dispatch_kernel(base_ref, cnt_ref, hn_ref, mask_ref, xs_in_ref, xs_ref, pos_ref, rank_ref, buf_ref, sem_ref):
    del xs_in_ref
    i = pl.program_id(0)
    tb = hn_ref.shape[0]
    r_i = lax.broadcasted_iota(I32, (tb, tb), 0)
    c_i = lax.broadcasted_iota(I32, (tb, tb), 1)
    incl = jnp.dot(mask_ref[...].astype(BF16), (r_i <= c_i).astype(BF16), preferred_element_type=F32)
    rank_ref[...] = incl - mask_ref[...]
    strip_row = lax.broadcasted_iota(I32, (STRIP, tb), 0)

    def copy(e, k, slot):
        start = pl.multiple_of(base_ref[i * N_EXPERTS + e] + k * STRIP, BF16_ROWS)
        return pltpu.make_async_copy(buf_ref.at[slot], xs_ref.at[e, pl.ds(start, STRIP), :], sem_ref.at[slot])

    def expert(e, nissued):
        rank = rank_ref[pl.ds(e, 1), :]
        sel = mask_ref[pl.ds(e, 1), :] > 0.0
        pos_ref[pl.ds(e, 1), :] = rank + base_ref[i * N_EXPERTS + e].astype(F32)
        nstrip = (cnt_ref[i * N_EXPERTS + e] + STRIP - 1) // STRIP

        def strip(k, nissued):
            slot = nissued % 2

            @pl.when(nissued >= 2)
            def _():
                copy(e, k, slot).wait()
            onehot = ((strip_row + k * STRIP).astype(F32) == rank) & sel
            buf_ref[slot] = jnp.dot(onehot.astype(BF16), hn_ref[...], preferred_element_type=F32).astype(BF16)
            copy(e, k, slot).start()
            return nissued + 1
        return lax.fori_loop(0, nstrip, strip, nissued)

    nissued = lax.fori_loop(0, N_EXPERTS, expert, jnp.int32(0))

    @pl.when(nissued >= 2)
    def _():
        copy(0, 0, nissued % 2).wait()

    @pl.when(nissued >= 1)
    def _():
        copy(0, 0, (nissued + 1) % 2).wait()


def _dispatch(hn, mask_t, base, cnt, rows_per_expert):
    t = hn.shape[0]
    tb = TB_DISPATCH
    xs0 = jnp.zeros((N_EXPERTS, rows_per_expert, D_MODEL), BF16)
    grid_spec = pltpu.PrefetchScalarGridSpec(
        num_scalar_prefetch=2,
        grid=(t // tb,),
        in_specs=[pl.BlockSpec((tb, D_MODEL), lambda i, *_: (i, 0)),
                  pl.BlockSpec((N_EXPERTS, tb), lambda i, *_: (0, i)),
                  pl.BlockSpec(memory_space=pl.ANY)],
        out_specs=[pl.BlockSpec(memory_space=pl.ANY),
                   pl.BlockSpec((N_EXPERTS, tb), lambda i, *_: (0, i))],
        scratch_shapes=[pltpu.VMEM((N_EXPERTS, tb), F32), pltpu.VMEM((2, STRIP, D_MODEL), BF16),
                        pltpu.SemaphoreType.DMA((2,))],
    )
    return pl.pallas_call(
        _dispatch_kernel,
        grid_spec=grid_spec,
        out_shape=[jax.ShapeDtypeStruct(xs0.shape, BF16), jax.ShapeDtypeStruct((N_EXPERTS, t), F32)],
        input_output_aliases={4: 0},
        compiler_params=_cparams(("arbitrary",)),
        name="dispatch",
    )(base, cnt, hn, mask_t, xs0)


def _ffn_kernel(total_ref, xs_ref, wg_ref, wu_ref, wd_ref, ye_ref):
    e = pl.program_id(0)
    j = pl.program_id(1)
    active = j * TM_FFN < total_ref[e]

    @pl.when(active)
    def _():
        x = xs_ref[0]
        g = jnp.dot(x, wg_ref[0], preferred_element_type=F32)
        u = jnp.dot(x, wu_ref[0], preferred_element_type=F32)
        a = (_silu(g) * u).astype(BF16)
        ye_ref[0] = jnp.dot(a, wd_ref[0], preferred_element_type=F32).astype(ye_ref.dtype)

    @pl.when(jnp.logical_not(active))
    def _():
        ye_ref[...] = jnp.zeros_like(ye_ref)


def _ffn(total, xs, wg, wu, wd):
    rows = xs.shape[1]
    once = pl.Buffered(1)
    grid_spec = pltpu.PrefetchScalarGridSpec(
        num_scalar_prefetch=1,
        grid=(N_EXPERTS, rows // TM_FFN),
        in_specs=[pl.BlockSpec((1, TM_FFN, D_MODEL), lambda e, j, *_: (e, j, 0)),
                  pl.BlockSpec((1, D_MODEL, EXPERT_FF), lambda e, j, *_: (e, 0, 0), pipeline_mode=once),
                  pl.BlockSpec((1, D_MODEL, EXPERT_FF), lambda e, j, *_: (e, 0, 0), pipeline_mode=once),
                  pl.BlockSpec((1, EXPERT_FF, D_MODEL), lambda e, j, *_: (e, 0, 0), pipeline_mode=once)],
        out_specs=pl.BlockSpec((1, TM_FFN, D_MODEL), lambda e, j, *_: (e, j, 0)),
    )
    return pl.pallas_call(
        _ffn_kernel,
        grid_spec=grid_spec,
        out_shape=jax.ShapeDtypeStruct(xs.shape, BF16),
        compiler_params=_cparams(("parallel", "parallel")),
        name="expert_ffn",
    )(total, xs, wg, wu, wd)


def _combine_kernel(start_ref, nstrip_ref, x1_ref, pos_ref, gate_ref, pe_ref, ye_ref, wple_ref, wpg_ref,
                    gple_ref, gpg_ref, gfin_ref, o_ref, acc_ref, buf_ref, sem_ref, xbuf_ref, xsem_ref):
    i = pl.program_id(0)
    tb = x1_ref.shape[0]
    lane = lax.broadcasted_iota(I32, (tb, STRIP), 1)

    def first_copy(e):
        start = pl.multiple_of(start_ref[i * N_EXPERTS + e], BF16_ROWS)
        return pltpu.make_async_copy(ye_ref.at[e, pl.ds(start, STRIP), :], buf_ref.at[e], sem_ref.at[e])

    for e in range(N_EXPERTS):
        @pl.when(nstrip_ref[i * N_EXPERTS + e] > 0)
        def _(e=e):
            first_copy(e).start()

    acc_ref[...] = x1_ref[...]
    for e in range(N_EXPERTS):
        n_e = nstrip_ref[i * N_EXPERTS + e]
        rel = pos_ref[:, e:e + 1] - start_ref[i * N_EXPERTS + e].astype(F32)
        gate = gate_ref[:, e:e + 1]
        live = gate != 0.0

        def expand(k, strip):
            onehot = ((lane + k * STRIP).astype(F32) == rel) & live
            return jnp.dot(onehot.astype(BF16), strip, preferred_element_type=F32) * gate

        @pl.when(n_e > 0)
        def _(e=e, expand=expand):
            first_copy(e).wait()
            acc_ref[...] += expand(0, buf_ref[e])

        def extra(k, carry, e=e, expand=expand):
            start = pl.multiple_of(start_ref[i * N_EXPERTS + e] + k * STRIP, BF16_ROWS)
            cp = pltpu.make_async_copy(ye_ref.at[e, pl.ds(start, STRIP), :], xbuf_ref, xsem_ref.at[0])
            cp.start()
            cp.wait()
            acc_ref[...] += expand(k, xbuf_ref[...])
            return carry
        lax.fori_loop(1, jnp.maximum(n_e, 1), extra, 0)

    x2 = acc_ref[...]
    ple = _rms(jnp.dot(pe_ref[...].astype(BF16), wple_ref[...], preferred_element_type=F32), gple_ref[...])
    pg = _sigmoid(jnp.dot(_rms(x2, gpg_ref[...]).astype(BF16), wpg_ref[...], preferred_element_type=F32))
    o_ref[...] = _rms(x2 + pg * ple, gfin_ref[...])


def _combine(start, nstrip, x1, pos, gate, pe2, ye, wple, wpg, g_ple, g_pg, g_final):
    t = x1.shape[0]
    tb = TB_COMBINE
    full = lambda a: pl.BlockSpec(a.shape, lambda i, *_: (0,) * a.ndim)
    grid_spec = pltpu.PrefetchScalarGridSpec(
        num_scalar_prefetch=2,
        grid=(t // tb,),
        in_specs=[pl.BlockSpec((tb, D_MODEL), lambda i, *_: (i, 0)),
                  pl.BlockSpec((tb, N_EXPERTS), lambda i, *_: (i, 0)),
                  pl.BlockSpec((tb, N_EXPERTS), lambda i, *_: (i, 0)),
                  pl.BlockSpec((tb, PLE_DIM), lambda i, *_: (i, 0)),
                  pl.BlockSpec(memory_space=pl.ANY),
                  full(wple), full(wpg), full(g_ple), full(g_pg), full(g_final)],
        out_specs=pl.BlockSpec((tb, D_MODEL), lambda i, *_: (i, 0)),
        scratch_shapes=[pltpu.VMEM((tb, D_MODEL), F32),
                        pltpu.VMEM((N_EXPERTS, STRIP, D_MODEL), BF16), pltpu.SemaphoreType.DMA((N_EXPERTS,)),
                        pltpu.VMEM((STRIP, D_MODEL), BF16), pltpu.SemaphoreType.DMA((1,))],
    )
    return pl.pallas_call(
        _combine_kernel,
        grid_spec=grid_spec,
        out_shape=jax.ShapeDtypeStruct((t, D_MODEL), F32),
        compiler_params=_cparams(("arbitrary",)),
        name="combine",
    )(start, nstrip, x1, pos, gate, pe2, ye, wple, wpg, g_ple, g_pg, g_final)


def _round_up(v, m):
    return (v + m - 1) // m * m


def _moe(x1, hn, aff_t, pe2, w):
    t = x1.shape[0]
    cap = max(1, CAPACITY_FACTOR * t // N_EXPERTS)
    mask_t, gate_t = _select(aff_t, cap)
    nt = t // TB_DISPATCH
    cnt = jnp.sum(mask_t.reshape(N_EXPERTS, nt, TB_DISPATCH), axis=2).astype(I32).T
    seg = (cnt + BF16_ROWS - 1) // BF16_ROWS * BF16_ROWS
    ends = jnp.cumsum(seg, axis=0)
    base = ends - seg
    total = ends[-1]
    rows = _round_up(cap + BF16_ROWS * nt + STRIP, TM_FFN)
    xs, pos_t = _dispatch(hn, mask_t, base.reshape(-1), cnt.reshape(-1), rows)
    ye = _ffn(total, xs, w["wg"], w["wu"], w["wd"])
    ntc = t // TB_COMBINE
    first = pos_t[:, ::TB_COMBINE].astype(I32).T
    ccnt = jnp.sum(mask_t.reshape(N_EXPERTS, ntc, TB_COMBINE), axis=2).astype(I32).T
    start = first // BF16_ROWS * BF16_ROWS
    nstrip = jnp.where(ccnt > 0, (first - start + ccnt + STRIP - 1) // STRIP, 0)
    return _combine(start.reshape(-1), nstrip.reshape(-1), x1, pos_t.T, gate_t.T, pe2, ye,
                    w["wple"], w["wpg"], w["g_ple"], w["g_pg"], w["g_final"])


def _trunk(x, pe, w):
    b, s, _ = x.shape
    t = b * s
    x2 = x.reshape(t, D_MODEL)
    qkv, z, xbc, dt = _in_proj(x2, w["g_mix"], w["wqkv"], w["wz"], w["wxbc"], w["wdt"])
    attn = _attention(qkv.reshape(b, s, 3 * ATTN_WIDTH), w["bias_tabs"])
    ssd = _ssd(xbc.reshape(b, s, CONV_DIM), z.reshape(b, s, SSD_INNER),
               dt.reshape(SSD_GROUPS, b, s, 2 * HEADS_PER_GROUP),
               w["conv_w"], w["conv_b"], w["dtb"], w["alog"], w["dsk"], w["gn"])
    x1, hn, aff_t = _out_proj(x2, attn.reshape(t, ATTN_WIDTH), ssd.reshape(t, SSD_INNER),
                              w["wo_a"], w["wo_s"], w["g_ffn"], w["wr_t"])
    y = _moe(x1, hn, aff_t, pe.reshape(t, PLE_DIM), w)
    return y.reshape(b, s, D_MODEL)


def _group_heads(a):
    parts = [jnp.concatenate([a[..., 0, g * HEADS_PER_GROUP:(g + 1) * HEADS_PER_GROUP],
                              a[..., 1, g * HEADS_PER_GROUP:(g + 1) * HEADS_PER_GROUP]], axis=-1)
             for g in range(SSD_GROUPS)]
    return jnp.stack(parts)


def _prepare(rel_bias, g_mix, w_in, conv_w, conv_b, dt_bias, a_log, d_skip, g_ssd, w_out, g_ffn, w_router,
             w_gate, w_up, w_down, g_pg, w_pg, w_ple, g_ple, g_final):
    row = lambda v: v.reshape(1, -1).astype(F32)
    c0 = 3 * ATTN_WIDTH
    c1 = c0 + SSD_INNER
    c2 = c1 + CONV_DIM
    w_dt = w_in[:, c2:].reshape(D_MODEL, 2, SSD_HEADS)
    return dict(
        g_mix=row(g_mix), wqkv=w_in[:, :c0].astype(BF16), wz=w_in[:, c0:c1].astype(BF16),
        wxbc=w_in[:, c1:c2].astype(BF16), wdt=_group_heads(w_dt).astype(BF16),
        bias_tabs=_attn_bias_tables(rel_bias),
        conv_w=conv_w.astype(F32), conv_b=row(conv_b),
        dtb=_group_heads(dt_bias.astype(F32)).reshape(SSD_GROUPS, 1, 2 * HEADS_PER_GROUP),
        alog=_group_heads(a_log.astype(F32)).reshape(SSD_GROUPS, 1, 2 * HEADS_PER_GROUP),
        dsk=row(jnp.repeat(d_skip, SSD_HEAD_DIM)), gn=row(g_ssd),
        wo_a=w_out[:ATTN_WIDTH].astype(BF16), wo_s=w_out[ATTN_WIDTH:].astype(BF16),
        g_ffn=row(g_ffn), wr_t=w_router.T.astype(F32),
        wg=w_gate.astype(BF16), wu=w_up.astype(BF16), wd=w_down.astype(BF16),
        g_pg=row(g_pg), wpg=w_pg.astype(BF16), wple=w_ple.astype(BF16), g_ple=row(g_ple), g_final=row(g_final),
    )


def kernel(x_prompt, x_sample, p_prompt, p_sample, rel_bias, g_mix, w_in, conv_w, conv_b, dt_bias, a_log, d_skip,
           g_ssd, w_out, g_ffn, w_router, w_gate, w_up, w_down, g_pg, w_pg, w_ple, g_ple, g_final):
    assert g_mix.shape[0] == 1, "single-layer trunk"
    w = _prepare(rel_bias, g_mix[0], w_in[0], conv_w[0], conv_b[0], dt_bias[0], a_log[0], d_skip[0], g_ssd[0],
                 w_out[0], g_ffn[0], w_router[0], w_gate[0], w_up[0], w_down[0], g_pg[0], w_pg[0], w_ple[0],
                 g_ple[0], g_final)
    return _trunk(x_prompt, p_prompt[0], w), _trunk(x_sample, p_sample[0], w)
```

```python
import functools
import math

import jax
import jax.numpy as jnp
import numpy as np
from jax import lax
from jax.experimental import pallas as pl
from jax.experimental.pallas import tpu as pltpu

F32 = jnp.float32
BF16 = jnp.bfloat16
I32 = jnp.int32

D_MODEL = 1024
PLE_DIM = 256
N_ATTN_HEADS = 8
ATTN_HEAD_DIM = 64
ATTN_WIDTH = N_ATTN_HEADS * ATTN_HEAD_DIM
DILATED_PATTERNS = ((128, 1), (512, 4), (2048, 16))
REL_BUCKETS = 32
REL_MAX_DISTANCE = 1024
SSD_HEADS = 8
SSD_HEAD_DIM = 64
SSD_INNER = SSD_HEADS * SSD_HEAD_DIM
SSD_GROUPS = 2
SSD_STATE = 128
SSD_CONV = 5
SSD_CHUNK = 128
CONV_DIM = SSD_INNER + 2 * SSD_GROUPS * SSD_STATE
N_EXPERTS = 16
EXPERT_FF = 2816
CAPACITY_FACTOR = 2
EPS = 1e-6

LANES = 128
SUBLANES = 8
BF16_ROWS = 16
VMEM_LIMIT = 56 * 1024 * 1024
NEG = -1e30

HEADS_PER_GROUP = SSD_HEADS // SSD_GROUPS
GROUP_COLS = SSD_INNER // SSD_GROUPS
ATT_HALF = 64
ATT_BQ = 128
ATT_KW = 256
TM_PROJ = 512
ATT_UNROLL = 8
TB_ROUTE = 256
STRIP = 64
TM_FFN = 512
SEL_CHUNK = 1024


def _cparams(sem):
    return pltpu.CompilerParams(dimension_semantics=sem, vmem_limit_bytes=VMEM_LIMIT)


def _rms(x, g):
    return x * lax.rsqrt(jnp.mean(x * x, axis=-1, keepdims=True) + EPS) * g


def _sigmoid(x):
    return 1.0 / (1.0 + jnp.exp(-x))


def _silu(x):
    return x * _sigmoid(x)


def _in_proj_kernel(x_ref, g_ref, wqkv_ref, wz_ref, wxbc_ref, wdt_ref, qkv_ref, z_ref, xbc_ref, dt_ref):
    h = _rms(x_ref[...], g_ref[...]).astype(BF16)
    qkv_ref[...] = jnp.dot(h, wqkv_ref[...], preferred_element_type=F32)
    z_ref[...] = jnp.dot(h, wz_ref[...], preferred_element_type=F32)
    xbc_ref[...] = jnp.dot(h, wxbc_ref[...], preferred_element_type=F32)
    for g in range(SSD_GROUPS):
        dt_ref[g] = jnp.dot(h, wdt_ref[g], preferred_element_type=F32)


def _in_proj(x2, g_mix, wqkv, wz, wxbc, wdt):
    t = x2.shape[0]
    tm = TM_PROJ
    full = lambda a: pl.BlockSpec(a.shape, lambda i: (0,) * a.ndim)
    return pl.pallas_call(
        _in_proj_kernel,
        grid=(t // tm,),
        in_specs=[pl.BlockSpec((tm, D_MODEL), lambda i: (i, 0)), full(g_mix), full(wqkv), full(wz), full(wxbc), full(wdt)],
        out_specs=[
            pl.BlockSpec((tm, 3 * ATTN_WIDTH), lambda i: (i, 0)),
            pl.BlockSpec((tm, SSD_INNER), lambda i: (i, 0)),
            pl.BlockSpec((tm, CONV_DIM), lambda i: (i, 0)),
            pl.BlockSpec((SSD_GROUPS, tm, 2 * HEADS_PER_GROUP), lambda i: (0, i, 0)),
        ],
        out_shape=[
            jax.ShapeDtypeStruct((t, 3 * ATTN_WIDTH), F32),
            jax.ShapeDtypeStruct((t, SSD_INNER), F32),
            jax.ShapeDtypeStruct((t, CONV_DIM), F32),
            jax.ShapeDtypeStruct((SSD_GROUPS, t, 2 * HEADS_PER_GROUP), F32),
        ],
        compiler_params=_cparams(("parallel",)),
        name="in_proj",
    )(x2, g_mix, wqkv, wz, wxbc, wdt)


def _t5_bucket(rel):
    half = REL_BUCKETS // 2
    max_exact = half // 2
    n = np.abs(rel)
    large = max_exact + (np.log(np.maximum(n, 1) / max_exact) / math.log(REL_MAX_DISTANCE / max_exact) * (half - max_exact)).astype(np.int32)
    large = np.minimum(large, half - 1)
    return (np.where(rel > 0, half, 0) + np.where(n < max_exact, n, large)).astype(np.int32)


_ATT_WINDOW_OFFSETS = (0, -ATT_HALF, -(ATT_KW - ATT_BQ))


def _attn_bias_tables(rel_bias):
    buckets = []
    for window, dilation in DILATED_PATTERNS:
        assert window // (2 * dilation) == ATT_HALF
        for off in _ATT_WINDOW_OFFSETS:
            m = np.arange(ATT_KW)[None, :] + off - np.arange(ATT_BQ)[:, None]
            buckets.append(np.where(np.abs(m) <= ATT_HALF, _t5_bucket(np.clip(m, -ATT_HALF, ATT_HALF) * dilation), -1))
    bucket = jnp.asarray(np.stack(buckets).reshape(len(DILATED_PATTERNS), len(_ATT_WINDOW_OFFSETS), 1, ATT_BQ, ATT_KW))
    tab = jnp.full(bucket.shape[:2] + (N_ATTN_HEADS, ATT_BQ, ATT_KW), NEG, F32)
    for bkt in range(REL_BUCKETS):
        tab = jnp.where(bucket == bkt, rel_bias[bkt].astype(F32)[None, None, :, None, None], tab)
    return tab


def _attn_kernel(q_ref, k_ref, v_ref, bias_ref, o_ref, acc_ref, m_ref, l_ref, *, s):
    lane = lax.broadcasted_iota(I32, (ATT_BQ, LANES), 1)
    first_head = lane < ATTN_HEAD_DIM
    scale = ATTN_HEAD_DIM ** -0.5
    nblk = s // ATT_BQ

    def rows(start, size, d):
        return pl.ds(start, size) if d == 1 else pl.ds(start, size, stride=d)

    def block(p, d, f):
        n = s // d
        nq = n // ATT_BQ
        kw = min(ATT_KW, n)
        r = f // nq
        q0 = (f % nq) * ATT_BQ
        ks = jnp.clip(q0 - ATT_HALF, 0, n - kw)
        kind = (q0 - ks) // ATT_HALF
        q_rows = rows(r + d * q0, ATT_BQ, d)
        k_rows = rows(r + d * ks, kw, d)
        qb = q_ref[0, q_rows, :] * scale
        kb = k_ref[0, k_rows, :].astype(BF16)
        vb = v_ref[0, k_rows, :].astype(BF16)
        outs, ms, ls = [], [], []
        for hh in range(2):
            sel = first_head if hh == 0 else jnp.logical_not(first_head)
            qm = jnp.where(sel, qb, 0.0).astype(BF16)
            sc = lax.dot_general(qm, kb, (((1,), (1,)), ((), ())), preferred_element_type=F32)
            sc = sc + bias_ref[p, kind, hh, :, :kw]
            mx = jnp.max(sc, axis=-1, keepdims=True)
            pe = jnp.exp(sc - mx)
            ls.append(jnp.sum(pe, axis=-1, keepdims=True))
            ms.append(mx)
            outs.append(jnp.dot(pe.astype(BF16), vb, preferred_element_type=F32))
        acc_ref[p, q_rows, :] = jnp.where(first_head, outs[0], outs[1])
        m_ref[p, q_rows, :] = jnp.where(first_head, ms[0], ms[1])
        l_ref[p, q_rows, :] = jnp.where(first_head, ls[0], ls[1])

    for p, (_, d) in enumerate(DILATED_PATTERNS):
        def body(it, carry, p=p, d=d):
            for u in range(ATT_UNROLL):
                block(p, d, it * ATT_UNROLL + u)
            return carry
        lax.fori_loop(0, nblk // ATT_UNROLL, body, 0)

    def finish(i, carry):
        sl = pl.ds(pl.multiple_of(i * ATT_BQ, ATT_BQ), ATT_BQ)
        ms = [m_ref[p, sl, :] for p in range(len(DILATED_PATTERNS))]
        mx = functools.reduce(jnp.maximum, ms)
        ws = [jnp.exp(m - mx) for m in ms]
        num = sum(w * acc_ref[p, sl, :] for p, w in enumerate(ws))
        den = sum(w * l_ref[p, sl, :] for p, w in enumerate(ws))
        o_ref[0, sl, :] = (num / den).astype(o_ref.dtype)
        return carry
    lax.fori_loop(0, nblk, finish, 0)


def _attention(qkv3, bias_tabs):
    b, s, _ = qkv3.shape
    assert s % (16 * ATT_BQ) == 0
    npairs = N_ATTN_HEADS // 2
    col = lambda base: pl.BlockSpec((1, s, LANES), lambda bi, hp: (bi, 0, base + hp))
    return pl.pallas_call(
        functools.partial(_attn_kernel, s=s),
        grid=(b, npairs),
        in_specs=[col(0), col(npairs), col(2 * npairs),
                  pl.BlockSpec((len(DILATED_PATTERNS), len(_ATT_WINDOW_OFFSETS), 2, ATT_BQ, ATT_KW), lambda bi, hp: (0, 0, hp, 0, 0))],
        out_specs=pl.BlockSpec((1, s, LANES), lambda bi, hp: (bi, 0, hp)),
        out_shape=jax.ShapeDtypeStruct((b, s, ATTN_WIDTH), BF16),
        scratch_shapes=[pltpu.VMEM((len(DILATED_PATTERNS), s, LANES), F32)] * 3,
        compiler_params=_cparams(("parallel", "parallel")),
        name="attention",
    )(qkv3, qkv3, qkv3, bias_tabs)


def _expand_heads(v):
    rows = v.shape[0]
    head = lax.broadcasted_iota(I32, (rows, GROUP_COLS), 1) // SSD_HEAD_DIM
    out = jnp.broadcast_to(v[:, 0:1], (rows, GROUP_COLS))
    for j in range(1, HEADS_PER_GROUP):
        out = jnp.where(head == j, v[:, j:j + 1], out)
    return out


def _ssd_kernel(xs_ref, b_ref, c_ref, z_ref, dt_ref, wx_ref, wb_ref, wc_ref, bx_ref, bb_ref, bc_ref,
                dtb_ref, alog_ref, dsk_ref, gn_ref, o_ref, xc_ref, bcv_ref, ccv_ref, y_ref, h_ref, *, s):
    L = SSD_CHUNK
    nc = s // L
    row = lax.broadcasted_iota(I32, (L, L), 0)
    colm = lax.broadcasted_iota(I32, (L, L), 1)
    lower = row >= colm
    upper = row <= colm
    tril = lower.astype(F32)
    triu = upper.astype(F32)
    head_of_lane = lax.broadcasted_iota(I32, (L, GROUP_COLS), 1) // SSD_HEAD_DIM
    a_neg = -jnp.exp(alog_ref[0])
    dt_bias = dtb_ref[0]

    def conv(ref, w_ref, bias_ref, c, t0):
        main = ref[0, pl.ds(t0, L), :]
        p0 = pl.multiple_of(jnp.maximum(t0 - SUBLANES, 0), SUBLANES)
        n0 = pl.multiple_of(jnp.minimum(t0 + L, s - SUBLANES), SUBLANES)
        prev = ref[0, pl.ds(p0, SUBLANES), :] * jnp.where(c > 0, 1.0, 0.0)
        nxt = ref[0, pl.ds(n0, SUBLANES), :] * jnp.where(c < nc - 1, 1.0, 0.0)
        ext = jnp.concatenate([prev, main, nxt], axis=0)
        acc = jnp.broadcast_to(bias_ref[...], main.shape)
        for k in range(SSD_CONV):
            lo = SUBLANES - SSD_CONV // 2 + k
            acc = acc + w_ref[k:k + 1, :] * ext[lo:lo + L, :]
        return _silu(acc)

    def chunk(c, direction):
        t0 = pl.multiple_of(c * L, L)
        if direction == 0:
            xs = conv(xs_ref, wx_ref, bx_ref, c, t0)
            bm = conv(b_ref, wb_ref, bb_ref, c, t0)
            cm = conv(c_ref, wc_ref, bc_ref, c, t0)
            xc_ref[pl.ds(t0, L), :] = xs
            bcv_ref[pl.ds(t0, L), :] = bm
            ccv_ref[pl.ds(t0, L), :] = cm
        else:
            xs = xc_ref[pl.ds(t0, L), :]
            bm = bcv_ref[pl.ds(t0, L), :]
            cm = ccv_ref[pl.ds(t0, L), :]
        x = dt_ref[0, 0, pl.ds(t0, L), :] + dt_bias
        dt_all = jnp.maximum(x, 0.0) + jnp.log1p(jnp.exp(-jnp.abs(x)))
        a_all = dt_all * a_neg
        lo = direction * HEADS_PER_GROUP
        dt = dt_all[:, lo:lo + HEADS_PER_GROUP]
        a = a_all[:, lo:lo + HEADS_PER_GROUP]
        cum = jnp.dot(tril if direction == 0 else triu, a, preferred_element_type=F32, precision=lax.Precision.HIGHEST)
        total = cum[L - 1:L, :] if direction == 0 else cum[0:1, :]
        mask = lower if direction == 0 else upper
        bmb = bm.astype(BF16)
        cmb = cm.astype(BF16)
        cb = lax.dot_general(cmb, bmb, (((1,), (1,)), ((), ())), preferred_element_type=F32)
        xdt = (xs * _expand_heads(dt)).astype(BF16)
        y = jnp.zeros((L, GROUP_COLS), F32)
        for j in range(HEADS_PER_GROUP):
            colb = jnp.broadcast_to(cum[:, j:j + 1], (L, L))
            seg = colb - colb.T
            dec = jnp.exp(jnp.where(mask, seg, NEG))
            yj = jnp.dot((cb * dec).astype(BF16), xdt, preferred_element_type=F32)
            y = jnp.where(head_of_lane == j, yj, y)
        hprev = h_ref[...]
        y = y + jnp.dot(cmb, hprev.astype(BF16), preferred_element_type=F32) * _expand_heads(jnp.exp(cum))
        xw = (xs * _expand_heads(dt * jnp.exp(total - cum))).astype(BF16)
        st = lax.dot_general(bmb, xw, (((0,), (0,)), ((), ())), preferred_element_type=F32)
        h_ref[...] = hprev * _expand_heads(jnp.exp(total)) + st
        if direction == 0:
            y_ref[pl.ds(t0, L), :] = y
        else:
            y = y_ref[pl.ds(t0, L), :] + y + dsk_ref[...] * xs
            y = y * _silu(z_ref[0, pl.ds(t0, L), :])
            o_ref[0, pl.ds(t0, L), :] = _rms(y, gn_ref[...]).astype(o_ref.dtype)

    h_ref[...] = jnp.zeros_like(h_ref)

    def fwd(c, carry):
        chunk(c, 0)
        return carry
    lax.fori_loop(0, nc, fwd, 0)
    h_ref[...] = jnp.zeros_like(h_ref)

    def bwd(i, carry):
        chunk(nc - 1 - i, 1)
        return carry
    lax.fori_loop(0, nc, bwd, 0)


def _ssd(xbc3, z3, dt4, conv_w, conv_b, dtb, alog, dsk, gn):
    b, s, _ = xbc3.shape
    g_off_b = SSD_INNER // SSD_STATE
    g_off_c = g_off_b + SSD_GROUPS
    xcol = pl.BlockSpec((1, s, GROUP_COLS), lambda bi, g: (bi, 0, g))
    bcol = pl.BlockSpec((1, s, SSD_STATE), lambda bi, g: (bi, 0, g_off_b + g))
    ccol = pl.BlockSpec((1, s, SSD_STATE), lambda bi, g: (bi, 0, g_off_c + g))
    return pl.pallas_call(
        functools.partial(_ssd_kernel, s=s),
        grid=(b, SSD_GROUPS),
        in_specs=[
            xcol, bcol, ccol,
            pl.BlockSpec((1, s, GROUP_COLS), lambda bi, g: (bi, 0, g)),
            pl.BlockSpec((1, 1, s, 2 * HEADS_PER_GROUP), lambda bi, g: (g, bi, 0, 0)),
            pl.BlockSpec((SSD_CONV, GROUP_COLS), lambda bi, g: (0, g)),
            pl.BlockSpec((SSD_CONV, SSD_STATE), lambda bi, g: (0, g_off_b + g)),
            pl.BlockSpec((SSD_CONV, SSD_STATE), lambda bi, g: (0, g_off_c + g)),
            pl.BlockSpec((1, GROUP_COLS), lambda bi, g: (0, g)),
            pl.BlockSpec((1, SSD_STATE), lambda bi, g: (0, g_off_b + g)),
            pl.BlockSpec((1, SSD_STATE), lambda bi, g: (0, g_off_c + g)),
            pl.BlockSpec((1, 1, 2 * HEADS_PER_GROUP), lambda bi, g: (g, 0, 0)),
            pl.BlockSpec((1, 1, 2 * HEADS_PER_GROUP), lambda bi, g: (g, 0, 0)),
            pl.BlockSpec((1, GROUP_COLS), lambda bi, g: (0, g)),
            pl.BlockSpec((1, GROUP_COLS), lambda bi, g: (0, g)),
        ],
        out_specs=pl.BlockSpec((1, s, GROUP_COLS), lambda bi, g: (bi, 0, g)),
        out_shape=jax.ShapeDtypeStruct((b, s, SSD_INNER), BF16),
        scratch_shapes=[
            pltpu.VMEM((s, GROUP_COLS), F32), pltpu.VMEM((s, SSD_STATE), F32), pltpu.VMEM((s, SSD_STATE), F32),
            pltpu.VMEM((s, GROUP_COLS), F32), pltpu.VMEM((SSD_STATE, GROUP_COLS), F32),
        ],
        compiler_params=_cparams(("parallel", "parallel")),
        name="ssd",
    )(xbc3, xbc3, xbc3, z3, dt4, conv_w, conv_w, conv_w, conv_b, conv_b, conv_b, dtb, alog, dsk, gn)


def _out_proj_kernel(x_ref, attn_ref, ssd_ref, wa_ref, ws_ref, g_ref, wr_ref, x1_ref, hn_ref, aff_ref):
    x1 = x_ref[...] + jnp.dot(attn_ref[...], wa_ref[...], preferred_element_type=F32) \
        + jnp.dot(ssd_ref[...], ws_ref[...], preferred_element_type=F32)
    x1_ref[...] = x1
    hn = _rms(x1, g_ref[...])
    hn_ref[...] = hn.astype(BF16)
    logits = lax.dot_general(wr_ref[...], hn, (((1,), (1,)), ((), ())), preferred_element_type=F32,
                             precision=lax.Precision.HIGHEST)
    e = jnp.exp(logits - jnp.max(logits, axis=0, keepdims=True))
    aff_ref[...] = e / jnp.sum(e, axis=0, keepdims=True)


def _out_proj(x2, attn2, ssd2, wa, ws, g_ffn, wr_t):
    t = x2.shape[0]
    tm = TM_PROJ
    full = lambda a: pl.BlockSpec(a.shape, lambda i: (0,) * a.ndim)
    return pl.pallas_call(
        _out_proj_kernel,
        grid=(t // tm,),
        in_specs=[pl.BlockSpec((tm, D_MODEL), lambda i: (i, 0)),
                  pl.BlockSpec((tm, ATTN_WIDTH), lambda i: (i, 0)),
                  pl.BlockSpec((tm, SSD_INNER), lambda i: (i, 0)),
                  full(wa), full(ws), full(g_ffn), full(wr_t)],
        out_specs=[pl.BlockSpec((tm, D_MODEL), lambda i: (i, 0)),
                   pl.BlockSpec((tm, D_MODEL), lambda i: (i, 0)),
                   pl.BlockSpec((N_EXPERTS, tm), lambda i: (0, i))],
        out_shape=[jax.ShapeDtypeStruct((t, D_MODEL), F32),
                   jax.ShapeDtypeStruct((t, D_MODEL), BF16),
                   jax.ShapeDtypeStruct((N_EXPERTS, t), F32)],
        compiler_params=_cparams(("parallel",)),
        name="out_proj",
    )(x2, attn2, ssd2, wa, ws, g_ffn, wr_t)


def _select_kernel(aff_ref, mask_ref, gate_ref, bits_ref, *, t, cap):
    nchunk = t // SEL_CHUNK
    lane = lax.broadcasted_iota(I32, (N_EXPERTS, SEL_CHUNK), 1)

    def to_bits(i, carry):
        sl = pl.ds(pl.multiple_of(i * SEL_CHUNK, SEL_CHUNK), SEL_CHUNK)
        bits_ref[:, sl] = lax.bitcast_convert_type(aff_ref[:, sl], I32)
        return carry
    lax.fori_loop(0, nchunk, to_bits, 0)

    def count(pred):
        def body(i, acc):
            sl = pl.ds(pl.multiple_of(i * SEL_CHUNK, SEL_CHUNK), SEL_CHUNK)
            return acc + pred(bits_ref[:, sl], lane + i * SEL_CHUNK).astype(F32)
        acc = lax.fori_loop(0, nchunk, body, jnp.zeros((N_EXPERTS, SEL_CHUNK), F32))
        return jnp.sum(acc, axis=1, keepdims=True)

    thr = jnp.zeros((N_EXPERTS, 1), I32)
    for bit in range(30, -1, -1):
        cand = thr | (1 << bit)
        cnt = count(lambda b, idx, cand=cand: b >= cand)
        thr = jnp.where(cnt >= cap, cand, thr)
    need = cap - count(lambda b, idx: b > thr)
    bound = jnp.zeros((N_EXPERTS, 1), I32)
    for bit in range(t.bit_length() - 1, -1, -1):
        cand = bound | (1 << bit)
        cnt = count(lambda b, idx, cand=cand: (b == thr) & (idx < cand))
        bound = jnp.where(cnt <= need, cand, bound)

    def emit(i, carry):
        sl = pl.ds(pl.multiple_of(i * SEL_CHUNK, SEL_CHUNK), SEL_CHUNK)
        b = bits_ref[:, sl]
        sel = (b > thr) | ((b == thr) & ((lane + i * SEL_CHUNK) < bound))
        mask_ref[:, sl] = sel.astype(F32)
        gate_ref[:, sl] = jnp.where(sel, aff_ref[:, sl], 0.0)
        return carry
    lax.fori_loop(0, nchunk, emit, 0)


def _select(aff_t, cap):
    t = aff_t.shape[1]
    full = pl.BlockSpec((N_EXPERTS, t), lambda i: (0, 0))
    return pl.pallas_call(
        functools.partial(_select_kernel, t=t, cap=cap),
        grid=(1,),
        in_specs=[full],
        out_specs=[full, full],
        out_shape=[jax.ShapeDtypeStruct((N_EXPERTS, t), F32)] * 2,
        scratch_shapes=[pltpu.VMEM((N_EXPERTS, t), I32)],
        compiler_params=_cparams(("arbitrary",)),
        name="select",
    )(aff_t)


def _dispatch_kernel(base_ref, cnt_ref, hn_ref, mask_ref, xs_ref, pos_ref,
                     rank_ref, onehot_ref, buf_ref, xbuf_ref, tail_ref, sem_ref, xsem_ref):
    i = pl.program_id(0)
    last = pl.num_programs(0) - 1
    tb = hn_ref.shape[0]
    slot = i % 2

    @pl.when(i == 0)
    def _():
        tail_ref[...] = jnp.zeros_like(tail_ref)
        xbuf_ref[...] = jnp.zeros_like(xbuf_ref)
        pad = [pltpu.make_async_copy(xbuf_ref, xs_ref.at[e, pl.ds(xs_ref.shape[1] - STRIP, STRIP), :], xsem_ref.at[0])
               for e in range(N_EXPERTS)]
        for cp in pad:
            cp.start()
        for cp in pad:
            cp.wait()

    r_i = lax.broadcasted_iota(I32, (tb, tb), 0)
    c_i = lax.broadcasted_iota(I32, (tb, tb), 1)
    incl = jnp.dot(mask_ref[...].astype(BF16), (r_i <= c_i).astype(BF16), preferred_element_type=F32)
    rank_ref[...] = incl - mask_ref[...]
    strip_row = lax.broadcasted_iota(I32, (STRIP, tb), 0).astype(F32)

    def onehot(e, k):
        off = (base_ref[i * N_EXPERTS + e] % BF16_ROWS).astype(F32)
        target = rank_ref[pl.ds(e, 1), :] + off
        sel = mask_ref[pl.ds(e, 1), :] > 0.0
        return (((strip_row + k * STRIP) == target) & sel).astype(BF16)

    def build(e, carry):
        pos_ref[pl.ds(e, 1), :] = rank_ref[pl.ds(e, 1), :] + base_ref[i * N_EXPERTS + e].astype(F32)
        onehot_ref[pl.ds(pl.multiple_of(e * STRIP, STRIP), STRIP), :] = onehot(e, 0)
        return carry
    lax.fori_loop(0, N_EXPERTS, build, 0)
    buf_ref[slot] = jnp.dot(onehot_ref[...], hn_ref[...], preferred_element_type=F32).astype(BF16)

    def strip_copy(s_, e, start):
        return pltpu.make_async_copy(buf_ref.at[s_, pl.ds(e * STRIP, STRIP), :],
                                     xs_ref.at[e, pl.ds(start, STRIP), :], sem_ref.at[s_, e])

    @pl.when(i > 0)
    def _():
        for e in range(N_EXPERTS):
            strip_copy(1 - slot, e, 0).wait()

    def finish(e, carry):
        base = base_ref[i * N_EXPERTS + e]
        off = base % BF16_ROWS
        end = off + cnt_ref[i * N_EXPERTS + e]
        row0 = pl.multiple_of(e * STRIP, STRIP)
        head = pl.ds(row0, BF16_ROWS)
        buf_ref[slot, head, :] = buf_ref[slot, head, :] + tail_ref[e]
        group = end // BF16_ROWS * BF16_ROWS
        has_tail = end % BF16_ROWS != 0

        @pl.when(jnp.logical_not(has_tail))
        def _():
            tail_ref[e] = jnp.zeros((BF16_ROWS, D_MODEL), BF16)

        @pl.when(has_tail & (group < STRIP))
        def _():
            tail_ref[e] = buf_ref[slot, pl.ds(pl.multiple_of(row0 + group, BF16_ROWS), BF16_ROWS), :]

        def extra(k, carry):
            xbuf_ref[...] = jnp.dot(onehot(e, k), hn_ref[...], preferred_element_type=F32).astype(BF16)

            @pl.when(has_tail & (group // STRIP == k))
            def _():
                tail_ref[e] = xbuf_ref[pl.ds(pl.multiple_of(group - k * STRIP, BF16_ROWS), BF16_ROWS), :]
            start = pl.multiple_of(base - off + k * STRIP, BF16_ROWS)
            cp = pltpu.make_async_copy(xbuf_ref, xs_ref.at[e, pl.ds(start, STRIP), :], xsem_ref.at[0])
            cp.start()
            cp.wait()
            return carry
        lax.fori_loop(1, jnp.maximum((end + STRIP - 1) // STRIP, 1), extra, 0)
        return carry
    lax.fori_loop(0, N_EXPERTS, finish, 0)

    for e in range(N_EXPERTS):
        base = base_ref[i * N_EXPERTS + e]
        strip_copy(slot, e, pl.multiple_of(base - base % BF16_ROWS, BF16_ROWS)).start()

    @pl.when(i == last)
    def _():
        for e in range(N_EXPERTS):
            strip_copy(slot, e, 0).wait()


def _dispatch(hn, mask_t, base, cnt, cap):
    t = hn.shape[0]
    tb = TB_ROUTE
    grid_spec = pltpu.PrefetchScalarGridSpec(
        num_scalar_prefetch=2,
        grid=(t // tb,),
        in_specs=[pl.BlockSpec((tb, D_MODEL), lambda i, *_: (i, 0)),
                  pl.BlockSpec((N_EXPERTS, tb), lambda i, *_: (0, i))],
        out_specs=[pl.BlockSpec(memory_space=pl.ANY),
                   pl.BlockSpec((N_EXPERTS, tb), lambda i, *_: (0, i))],
        scratch_shapes=[pltpu.VMEM((N_EXPERTS, tb), F32),
                        pltpu.VMEM((N_EXPERTS * STRIP, tb), BF16),
                        pltpu.VMEM((2, N_EXPERTS * STRIP, D_MODEL), BF16),
                        pltpu.VMEM((STRIP, D_MODEL), BF16),
                        pltpu.VMEM((N_EXPERTS, BF16_ROWS, D_MODEL), BF16),
                        pltpu.SemaphoreType.DMA((2, N_EXPERTS)), pltpu.SemaphoreType.DMA((1,))],
    )
    return pl.pallas_call(
        _dispatch_kernel,
        grid_spec=grid_spec,
        out_shape=[jax.ShapeDtypeStruct((N_EXPERTS, cap + STRIP, D_MODEL), BF16),
                   jax.ShapeDtypeStruct((N_EXPERTS, t), F32)],
        compiler_params=_cparams(("arbitrary",)),
        name="dispatch",
    )(base, cnt, hn, mask_t)


def _ffn_kernel(xs_ref, wg_ref, wu_ref, wd_ref, ye_ref):
    pad_tile = pl.program_id(1) == pl.num_programs(1) - 1

    @pl.when(jnp.logical_not(pad_tile))
    def _():
        x = xs_ref[0]
        g = jnp.dot(x, wg_ref[0], preferred_element_type=F32)
        u = jnp.dot(x, wu_ref[0], preferred_element_type=F32)
        a = (_silu(g) * u).astype(BF16)
        ye_ref[0] = jnp.dot(a, wd_ref[0], preferred_element_type=F32).astype(ye_ref.dtype)

    @pl.when(pad_tile)
    def _():
        ye_ref[...] = jnp.zeros_like(ye_ref)


def _ffn(xs, wg, wu, wd, cap):
    ntile = cap // TM_FFN
    once = pl.Buffered(1)
    return pl.pallas_call(
        _ffn_kernel,
        grid=(N_EXPERTS, ntile + 1),
        in_specs=[pl.BlockSpec((1, TM_FFN, D_MODEL), lambda e, j: (e, jnp.minimum(j, ntile - 1), 0)),
                  pl.BlockSpec((1, D_MODEL, EXPERT_FF), lambda e, j: (e, 0, 0), pipeline_mode=once),
                  pl.BlockSpec((1, D_MODEL, EXPERT_FF), lambda e, j: (e, 0, 0), pipeline_mode=once),
                  pl.BlockSpec((1, EXPERT_FF, D_MODEL), lambda e, j: (e, 0, 0), pipeline_mode=once)],
        out_specs=pl.BlockSpec((1, TM_FFN, D_MODEL), lambda e, j: (e, j, 0)),
        out_shape=jax.ShapeDtypeStruct((N_EXPERTS, cap + TM_FFN, D_MODEL), BF16),
        compiler_params=_cparams(("parallel", "parallel")),
        name="expert_ffn",
    )(xs, wg, wu, wd)


def _combine_kernel(start_ref, nstrip_ref, x1_ref, pos_ref, gate_ref, pe_ref, ye_ref, wple_ref, wpg_ref,
                    gple_ref, gpg_ref, gfin_ref, o_ref, acc_ref, w_ref, buf_ref, sem_ref, xbuf_ref, xsem_ref):
    i = pl.program_id(0)
    ntile = pl.num_programs(0)
    tb = x1_ref.shape[0]
    slot = i % 2
    lane = lax.broadcasted_iota(I32, (tb, LANES), 1)
    left = lane < STRIP
    lane_f = lane.astype(F32)

    def strip_copy(s_, tile, e):
        start = pl.multiple_of(start_ref[tile * N_EXPERTS + e], BF16_ROWS)
        return pltpu.make_async_copy(ye_ref.at[e, pl.ds(start, STRIP), :],
                                     buf_ref.at[s_, pl.ds(e * STRIP, STRIP), :], sem_ref.at[s_, e])

    @pl.when(i == 0)
    def _():
        for e in range(N_EXPERTS):
            strip_copy(0, 0, e).start()

    @pl.when(i + 1 < ntile)
    def _():
        for e in range(N_EXPERTS):
            strip_copy(1 - slot, i + 1, e).start()

    def rel_gate(e, width=LANES):
        rel = pos_ref[:, e:e + 1] - start_ref[i * N_EXPERTS + e].astype(F32)
        return jnp.broadcast_to(rel, (tb, width)), jnp.broadcast_to(gate_ref[:, e:e + 1], (tb, width))

    assert 2 * STRIP == LANES
    for pair in range(N_EXPERTS // 2):
        rel0, g0 = rel_gate(2 * pair)
        rel1, g1 = rel_gate(2 * pair + 1)
        target = jnp.where(left, rel0, rel1 + STRIP)
        gate = jnp.where(left, g0, g1)
        w_ref[:, pair * LANES:(pair + 1) * LANES] = jnp.where(lane_f == target, gate, 0.0).astype(BF16)

    for e in range(N_EXPERTS):
        strip_copy(slot, i, e).wait()
    acc_ref[...] = x1_ref[...] + jnp.dot(w_ref[...], buf_ref[slot], preferred_element_type=F32)

    for e in range(N_EXPERTS):
        def extra(k, carry, e=e):
            start = pl.multiple_of(start_ref[i * N_EXPERTS + e] + k * STRIP, BF16_ROWS)
            cp = pltpu.make_async_copy(ye_ref.at[e, pl.ds(start, STRIP), :], xbuf_ref, xsem_ref.at[0])
            cp.start()
            rel, gate = rel_gate(e, STRIP)
            strip_lane = lax.broadcasted_iota(I32, (tb, STRIP), 1) + k * STRIP
            w = jnp.where(strip_lane.astype(F32) == rel, gate, 0.0).astype(BF16)
            cp.wait()
            acc_ref[...] += jnp.dot(w, xbuf_ref[...], preferred_element_type=F32)
            return carry
        lax.fori_loop(1, jnp.maximum(nstrip_ref[i * N_EXPERTS + e], 1), extra, 0)

    x2 = acc_ref[...]
    ple = _rms(jnp.dot(pe_ref[...].astype(BF16), wple_ref[...], preferred_element_type=F32), gple_ref[...])
    pg = _sigmoid(jnp.dot(_rms(x2, gpg_ref[...]).astype(BF16), wpg_ref[...], preferred_element_type=F32))
    o_ref[...] = _rms(x2 + pg * ple, gfin_ref[...])


def _combine(start, nstrip, x1, pos, gate, pe2, ye, wple, wpg, g_ple, g_pg, g_final):
    t = x1.shape[0]
    tb = TB_ROUTE
    full = lambda a: pl.BlockSpec(a.shape, lambda i, *_: (0,) * a.ndim)
    grid_spec = pltpu.PrefetchScalarGridSpec(
        num_scalar_prefetch=2,
        grid=(t // tb,),
        in_specs=[pl.BlockSpec((tb, D_MODEL), lambda i, *_: (i, 0)),
                  pl.BlockSpec((tb, N_EXPERTS), lambda i, *_: (i, 0)),
                  pl.BlockSpec((tb, N_EXPERTS), lambda i, *_: (i, 0)),
                  pl.BlockSpec((tb, PLE_DIM), lambda i, *_: (i, 0)),
                  pl.BlockSpec(memory_space=pl.ANY),
                  full(wple), full(wpg), full(g_ple), full(g_pg), full(g_final)],
        out_specs=pl.BlockSpec((tb, D_MODEL), lambda i, *_: (i, 0)),
        scratch_shapes=[pltpu.VMEM((tb, D_MODEL), F32),
                        pltpu.VMEM((tb, N_EXPERTS * STRIP), BF16),
                        pltpu.VMEM((2, N_EXPERTS * STRIP, D_MODEL), BF16), pltpu.SemaphoreType.DMA((2, N_EXPERTS)),
                        pltpu.VMEM((STRIP, D_MODEL), BF16), pltpu.SemaphoreType.DMA((1,))],
    )
    return pl.pallas_call(
        _combine_kernel,
        grid_spec=grid_spec,
        out_shape=jax.ShapeDtypeStruct((t, D_MODEL), F32),
        compiler_params=_cparams(("arbitrary",)),
        name="combine",
    )(start, nstrip, x1, pos, gate, pe2, ye, wple, wpg, g_ple, g_pg, g_final)


def _moe(x1, hn, aff_t, pe2, w):
    t = x1.shape[0]
    cap = max(1, CAPACITY_FACTOR * t // N_EXPERTS)
    assert cap % TM_FFN == 0 and t % TB_ROUTE == 0
    mask_t, gate_t = _select(aff_t, cap)
    nt = t // TB_ROUTE
    cnt = jnp.sum(mask_t.reshape(N_EXPERTS, nt, TB_ROUTE), axis=2).astype(I32).T
    base = jnp.cumsum(cnt, axis=0) - cnt
    xs, pos_t = _dispatch(hn, mask_t, base.reshape(-1), cnt.reshape(-1), cap)
    ye = _ffn(xs, w["wg"], w["wu"], w["wd"], cap)
    start = base // BF16_ROWS * BF16_ROWS
    nstrip = jnp.where(cnt > 0, (base - start + cnt + STRIP - 1) // STRIP, 0)
    return _combine(start.reshape(-1), nstrip.reshape(-1), x1, pos_t.T, gate_t.T, pe2, ye,
                    w["wple"], w["wpg"], w["g_ple"], w["g_pg"], w["g_final"])


def _trunk(x, pe, w):
    b, s, _ = x.shape
    t = b * s
    x2 = x.reshape(t, D_MODEL)
    qkv, z, xbc, dt = _in_proj(x2, w["g_mix"], w["wqkv"], w["wz"], w["wxbc"], w["wdt"])
    attn = _attention(qkv.reshape(b, s, 3 * ATTN_WIDTH), w["bias_tabs"])
    ssd = _ssd(xbc.reshape(b, s, CONV_DIM), z.reshape(b, s, SSD_INNER),
               dt.reshape(SSD_GROUPS, b, s, 2 * HEADS_PER_GROUP),
               w["conv_w"], w["conv_b"], w["dtb"], w["alog"], w["dsk"], w["gn"])
    x1, hn, aff_t = _out_proj(x2, attn.reshape(t, ATTN_WIDTH), ssd.reshape(t, SSD_INNER),
                              w["wo_a"], w["wo_s"], w["g_ffn"], w["wr_t"])
    y = _moe(x1, hn, aff_t, pe.reshape(t, PLE_DIM), w)
    return y.reshape(b, s, D_MODEL)


def _group_heads(a):
    parts = [jnp.concatenate([a[..., 0, g * HEADS_PER_GROUP:(g + 1) * HEADS_PER_GROUP],
                              a[..., 1, g * HEADS_PER_GROUP:(g + 1) * HEADS_PER_GROUP]], axis=-1)
             for g in range(SSD_GROUPS)]
    return jnp.stack(parts)


def _prepare(rel_bias, g_mix, w_in, conv_w, conv_b, dt_bias, a_log, d_skip, g_ssd, w_out, g_ffn, w_router,
             w_gate, w_up, w_down, g_pg, w_pg, w_ple, g_ple, g_final):
    row = lambda v: v.reshape(1, -1).astype(F32)
    c0 = 3 * ATTN_WIDTH
    c1 = c0 + SSD_INNER
    c2 = c1 + CONV_DIM
    w_dt = w_in[:, c2:].reshape(D_MODEL, 2, SSD_HEADS)
    return dict(
        g_mix=row(g_mix), wqkv=w_in[:, :c0].astype(BF16), wz=w_in[:, c0:c1].astype(BF16),
        wxbc=w_in[:, c1:c2].astype(BF16), wdt=_group_heads(w_dt).astype(BF16),
        bias_tabs=_attn_bias_tables(rel_bias),
        conv_w=conv_w.astype(F32), conv_b=row(conv_b),
        dtb=_group_heads(dt_bias.astype(F32)).reshape(SSD_GROUPS, 1, 2 * HEADS_PER_GROUP),
        alog=_group_heads(a_log.astype(F32)).reshape(SSD_GROUPS, 1, 2 * HEADS_PER_GROUP),
        dsk=row(jnp.repeat(d_skip, SSD_HEAD_DIM)), gn=row(g_ssd),
        wo_a=w_out[:ATTN_WIDTH].astype(BF16), wo_s=w_out[ATTN_WIDTH:].astype(BF16),
        g_ffn=row(g_ffn), wr_t=w_router.T.astype(F32),
        wg=w_gate.astype(BF16), wu=w_up.astype(BF16), wd=w_down.astype(BF16),
        g_pg=row(g_pg), wpg=w_pg.astype(BF16), wple=w_ple.astype(BF16), g_ple=row(g_ple), g_final=row(g_final),
    )


def kernel(x_prompt, x_sample, p_prompt, p_sample, rel_bias, g_mix, w_in, conv_w, conv_b, dt_bias, a_log, d_skip,
           g_ssd, w_out, g_ffn, w_router, w_gate, w_up, w_down, g_pg, w_pg, w_ple, g_ple, g_final):
    assert g_mix.shape[0] == 1, "single-layer trunk"
    w = _prepare(rel_bias, g_mix[0], w_in[0], conv_w[0], conv_b[0], dt_bias[0], a_log[0], d_skip[0], g_ssd[0],
                 w_out[0], g_ffn[0], w_router[0], w_gate[0], w_up[0], w_down[0], g_pg[0], w_pg[0], w_ple[0],
                 g_ple[0], g_final)
    return _trunk(x_prompt, p_prompt[0], w), _trunk(x_sample, p_sample[0], w)
```

```python
import functools
import math

import jax
import jax.numpy as jnp
import numpy as np
from jax import lax
from jax.experimental import pallas as pl
from jax.experimental.pallas import tpu as pltpu

F32 = jnp.float32
BF16 = jnp.bfloat16
I32 = jnp.int32

D_MODEL = 1024
PLE_DIM = 256
N_ATTN_HEADS = 8
ATTN_HEAD_DIM = 64
ATTN_WIDTH = N_ATTN_HEADS * ATTN_HEAD_DIM
DILATED_PATTERNS = ((128, 1), (512, 4), (2048, 16))
REL_BUCKETS = 32
REL_MAX_DISTANCE = 1024
SSD_HEADS = 8
SSD_HEAD_DIM = 64
SSD_INNER = SSD_HEADS * SSD_HEAD_DIM
SSD_GROUPS = 2
SSD_STATE = 128
SSD_CONV = 5
SSD_CHUNK = 128
CONV_DIM = SSD_INNER + 2 * SSD_GROUPS * SSD_STATE
N_EXPERTS = 16
EXPERT_FF = 2816
CAPACITY_FACTOR = 2
EPS = 1e-6

LANES = 128
SUBLANES = 8
BF16_ROWS = 16
VMEM_LIMIT = 56 * 1024 * 1024
NEG = -1e30

HEADS_PER_GROUP = SSD_HEADS // SSD_GROUPS
GROUP_COLS = SSD_INNER // SSD_GROUPS
ATT_HALF = 64
ATT_BQ = 128
ATT_KW = 256
TM_PROJ = 512
SSD_UNROLL = 2
ATT_UNROLL = 8
TB_ROUTE = 256
STRIP = 64
TM_FFN = 512
SEL_CHUNK = 1024


def _cparams(sem):
    return pltpu.CompilerParams(dimension_semantics=sem, vmem_limit_bytes=VMEM_LIMIT)


def _rms(x, g):
    return x * lax.rsqrt(jnp.mean(x * x, axis=-1, keepdims=True) + EPS) * g


def _sigmoid(x):
    return 0.5 * jnp.tanh(0.5 * x) + 0.5


def _silu(x):
    return x * _sigmoid(x)


def _in_proj_kernel(x_ref, g_ref, wqkv_ref, wz_ref, wxbc_ref, wdt_ref, wdtt_ref, qkv_ref, z_ref, xbc_ref, dt_ref, dtt_ref):
    h = _rms(x_ref[...], g_ref[...]).astype(BF16)
    qkv_ref[...] = jnp.dot(h, wqkv_ref[...], preferred_element_type=F32)
    z_ref[...] = jnp.dot(h, wz_ref[...], preferred_element_type=F32)
    xbc_ref[...] = jnp.dot(h, wxbc_ref[...], preferred_element_type=F32)
    for g in range(SSD_GROUPS):
        dt_ref[g] = jnp.dot(h, wdt_ref[g], preferred_element_type=F32)
        dtt_ref[g] = lax.dot_general(wdtt_ref[g], h, (((1,), (1,)), ((), ())), preferred_element_type=F32)


def _in_proj(x2, g_mix, wqkv, wz, wxbc, wdt, wdtt):
    t = x2.shape[0]
    tm = TM_PROJ
    full = lambda a: pl.BlockSpec(a.shape, lambda i: (0,) * a.ndim)
    return pl.pallas_call(
        _in_proj_kernel,
        grid=(t // tm,),
        in_specs=[pl.BlockSpec((tm, D_MODEL), lambda i: (i, 0)), full(g_mix), full(wqkv), full(wz), full(wxbc), full(wdt), full(wdtt)],
        out_specs=[
            pl.BlockSpec((tm, 3 * ATTN_WIDTH), lambda i: (i, 0)),
            pl.BlockSpec((tm, SSD_INNER), lambda i: (i, 0)),
            pl.BlockSpec((tm, CONV_DIM), lambda i: (i, 0)),
            pl.BlockSpec((SSD_GROUPS, tm, 2 * HEADS_PER_GROUP), lambda i: (0, i, 0)),
            pl.BlockSpec((SSD_GROUPS, 2 * HEADS_PER_GROUP, tm), lambda i: (0, 0, i)),
        ],
        out_shape=[
            jax.ShapeDtypeStruct((t, 3 * ATTN_WIDTH), F32),
            jax.ShapeDtypeStruct((t, SSD_INNER), F32),
            jax.ShapeDtypeStruct((t, CONV_DIM), F32),
            jax.ShapeDtypeStruct((SSD_GROUPS, t, 2 * HEADS_PER_GROUP), F32),
            jax.ShapeDtypeStruct((SSD_GROUPS, 2 * HEADS_PER_GROUP, t), F32),
        ],
        compiler_params=_cparams(("parallel",)),
        name="in_proj",
    )(x2, g_mix, wqkv, wz, wxbc, wdt, wdtt)


def _t5_bucket(rel):
    half = REL_BUCKETS // 2
    max_exact = half // 2
    n = np.abs(rel)
    large = max_exact + (np.log(np.maximum(n, 1) / max_exact) / math.log(REL_MAX_DISTANCE / max_exact) * (half - max_exact)).astype(np.int32)
    large = np.minimum(large, half - 1)
    return (np.where(rel > 0, half, 0) + np.where(n < max_exact, n, large)).astype(np.int32)


_ATT_WINDOW_OFFSETS = (0, -ATT_HALF, -(ATT_KW - ATT_BQ))


def _attn_bias_tables(rel_bias):
    buckets = []
    for window, dilation in DILATED_PATTERNS:
        assert window // (2 * dilation) == ATT_HALF
        for off in _ATT_WINDOW_OFFSETS:
            m = np.arange(ATT_KW)[None, :] + off - np.arange(ATT_BQ)[:, None]
            buckets.append(np.where(np.abs(m) <= ATT_HALF, _t5_bucket(np.clip(m, -ATT_HALF, ATT_HALF) * dilation), -1))
    bucket = jnp.asarray(np.stack(buckets).reshape(len(DILATED_PATTERNS), len(_ATT_WINDOW_OFFSETS), 1, ATT_BQ, ATT_KW))
    tab = jnp.full(bucket.shape[:2] + (N_ATTN_HEADS, ATT_BQ, ATT_KW), NEG, F32)
    for bkt in range(REL_BUCKETS):
        tab = jnp.where(bucket == bkt, rel_bias[bkt].astype(F32)[None, None, :, None, None], tab)
    return tab


def _attn_kernel(q_ref, k_ref, v_ref, bias_ref, o_ref, acc_ref, m_ref, l_ref, *, s):
    lane = lax.broadcasted_iota(I32, (ATT_BQ, LANES), 1)
    first_head = lane < ATTN_HEAD_DIM
    scale = ATTN_HEAD_DIM ** -0.5
    nblk = s // ATT_BQ

    def rows(start, size, d):
        return pl.ds(start, size) if d == 1 else pl.ds(start, size, stride=d)

    def block(p, d, f):
        n = s // d
        nq = n // ATT_BQ
        kw = min(ATT_KW, n)
        r = f // nq
        q0 = (f % nq) * ATT_BQ
        ks = jnp.clip(q0 - ATT_HALF, 0, n - kw)
        kind = (q0 - ks) // ATT_HALF
        q_rows = rows(r + d * q0, ATT_BQ, d)
        k_rows = rows(r + d * ks, kw, d)
        qb = q_ref[0, q_rows, :] * scale
        kb = k_ref[0, k_rows, :].astype(BF16)
        vb = v_ref[0, k_rows, :].astype(BF16)
        outs, ms, ls = [], [], []
        for hh in range(2):
            sel = first_head if hh == 0 else jnp.logical_not(first_head)
            qm = jnp.where(sel, qb, 0.0).astype(BF16)
            sc = lax.dot_general(qm, kb, (((1,), (1,)), ((), ())), preferred_element_type=F32)
            sc = sc + bias_ref[p, kind, hh, :, :kw]
            mx = jnp.max(sc, axis=-1, keepdims=True)
            pe = jnp.exp(sc - mx)
            ls.append(jnp.sum(pe, axis=-1, keepdims=True))
            ms.append(mx)
            outs.append(jnp.dot(pe.astype(BF16), vb, preferred_element_type=F32))
        acc_ref[p, q_rows, :] = jnp.where(first_head, outs[0], outs[1])
        m_ref[p, q_rows, :] = jnp.where(first_head, ms[0], ms[1])
        l_ref[p, q_rows, :] = jnp.where(first_head, ls[0], ls[1])

    for p, (_, d) in enumerate(DILATED_PATTERNS):
        def body(it, carry, p=p, d=d):
            for u in range(ATT_UNROLL):
                block(p, d, it * ATT_UNROLL + u)
            return carry
        lax.fori_loop(0, nblk // ATT_UNROLL, body, 0)

    def finish(i, carry):
        sl = pl.ds(pl.multiple_of(i * ATT_BQ, ATT_BQ), ATT_BQ)
        ms = [m_ref[p, sl, :] for p in range(len(DILATED_PATTERNS))]
        mx = functools.reduce(jnp.maximum, ms)
        ws = [jnp.exp(m - mx) for m in ms]
        num = sum(w * acc_ref[p, sl, :] for p, w in enumerate(ws))
        den = sum(w * l_ref[p, sl, :] for p, w in enumerate(ws))
        o_ref[0, sl, :] = (num / den).astype(o_ref.dtype)
        return carry
    lax.fori_loop(0, nblk, finish, 0)


def _attention(qkv3, bias_tabs):
    b, s, _ = qkv3.shape
    assert s % (16 * ATT_BQ) == 0
    npairs = N_ATTN_HEADS // 2
    col = lambda base: pl.BlockSpec((1, s, LANES), lambda bi, hp: (bi, 0, base + hp))
    return pl.pallas_call(
        functools.partial(_attn_kernel, s=s),
        grid=(b, npairs),
        in_specs=[col(0), col(npairs), col(2 * npairs),
                  pl.BlockSpec((len(DILATED_PATTERNS), len(_ATT_WINDOW_OFFSETS), 2, ATT_BQ, ATT_KW), lambda bi, hp: (0, 0, hp, 0, 0))],
        out_specs=pl.BlockSpec((1, s, LANES), lambda bi, hp: (bi, 0, hp)),
        out_shape=jax.ShapeDtypeStruct((b, s, ATTN_WIDTH), BF16),
        scratch_shapes=[pltpu.VMEM((len(DILATED_PATTERNS), s, LANES), F32)] * 3,
        compiler_params=_cparams(("parallel", "parallel")),
        name="attention",
    )(qkv3, qkv3, qkv3, bias_tabs)


def _expand_heads(v):
    rows = v.shape[0]
    head = lax.broadcasted_iota(I32, (rows, GROUP_COLS), 1) // SSD_HEAD_DIM
    out = jnp.broadcast_to(v[:, 0:1], (rows, GROUP_COLS))
    for j in range(1, HEADS_PER_GROUP):
        out = jnp.where(head == j, v[:, j:j + 1], out)
    return out


def _ssd_kernel(xs_ref, b_ref, c_ref, z_ref, dt_ref, dtt_ref, wx_ref, wb_ref, wc_ref, bx_ref, bb_ref, bc_ref,
                dtb_ref, alog_ref, dtbc_ref, alogc_ref, dsk_ref, gn_ref, o_ref, xc_ref, bcv_ref, ccv_ref, y_ref, h_ref, *, s):
    L = SSD_CHUNK
    nc = s // L
    row = lax.broadcasted_iota(I32, (L, L), 0)
    colm = lax.broadcasted_iota(I32, (L, L), 1)
    lower = row >= colm
    upper = row <= colm
    tril = lower.astype(F32)
    triu = upper.astype(F32)
    head_of_lane = lax.broadcasted_iota(I32, (L, GROUP_COLS), 1) // SSD_HEAD_DIM
    a_neg = -jnp.exp(alog_ref[0])
    dt_bias = dtb_ref[0]
    a_neg_col = -jnp.exp(alogc_ref[0])
    dt_bias_col = dtbc_ref[0]

    def conv(ref, w_ref, bias_ref, c, t0):
        main = ref[0, pl.ds(t0, L), :]
        p0 = pl.multiple_of(jnp.maximum(t0 - SUBLANES, 0), SUBLANES)
        n0 = pl.multiple_of(jnp.minimum(t0 + L, s - SUBLANES), SUBLANES)
        prev = ref[0, pl.ds(p0, SUBLANES), :] * jnp.where(c > 0, 1.0, 0.0)
        nxt = ref[0, pl.ds(n0, SUBLANES), :] * jnp.where(c < nc - 1, 1.0, 0.0)
        ext = jnp.concatenate([prev, main, nxt], axis=0)
        acc = jnp.broadcast_to(bias_ref[...], main.shape)
        for k in range(SSD_CONV):
            lo = SUBLANES - SSD_CONV // 2 + k
            acc = acc + w_ref[k:k + 1, :] * ext[lo:lo + L, :]
        return _silu(acc)

    def chunk(c, direction):
        t0 = pl.multiple_of(c * L, L)
        if direction == 0:
            xs = conv(xs_ref, wx_ref, bx_ref, c, t0)
            bm = conv(b_ref, wb_ref, bb_ref, c, t0)
            cm = conv(c_ref, wc_ref, bc_ref, c, t0)
            xc_ref[pl.ds(t0, L), :] = xs
            bcv_ref[pl.ds(t0, L), :] = bm
            ccv_ref[pl.ds(t0, L), :] = cm
        else:
            xs = xc_ref[pl.ds(t0, L), :]
            bm = bcv_ref[pl.ds(t0, L), :]
            cm = ccv_ref[pl.ds(t0, L), :]
        x = dt_ref[0, 0, pl.ds(t0, L), :] + dt_bias
        dt_all = jnp.maximum(x, 0.0) + jnp.log1p(jnp.exp(-jnp.abs(x)))
        a_all = dt_all * a_neg
        lo = direction * HEADS_PER_GROUP
        dt = dt_all[:, lo:lo + HEADS_PER_GROUP]
        a = a_all[:, lo:lo + HEADS_PER_GROUP]
        cum = jnp.dot(tril if direction == 0 else triu, a, preferred_element_type=F32, precision=lax.Precision.HIGHEST)
        total = cum[L - 1:L, :] if direction == 0 else cum[0:1, :]
        xr = dtt_ref[0, 0, :, pl.ds(t0, L)] + dt_bias_col
        a_row = (jnp.maximum(xr, 0.0) + jnp.log1p(jnp.exp(-jnp.abs(xr)))) * a_neg_col
        cum_row = jnp.dot(a_row, triu if direction == 0 else tril, preferred_element_type=F32,
                          precision=lax.Precision.HIGHEST)
        mask = lower if direction == 0 else upper
        bmb = bm.astype(BF16)
        cmb = cm.astype(BF16)
        cb = lax.dot_general(cmb, bmb, (((1,), (1,)), ((), ())), preferred_element_type=F32)
        xdt = (xs * _expand_heads(dt)).astype(BF16)
        y = jnp.zeros((L, GROUP_COLS), F32)
        for j in range(HEADS_PER_GROUP):
            colb = jnp.broadcast_to(cum[:, j:j + 1], (L, L))
            rowb = jnp.broadcast_to(cum_row[lo + j:lo + j + 1, :], (L, L))
            seg = colb - rowb
            dec = jnp.exp(jnp.where(mask, seg, NEG))
            yj = jnp.dot((cb * dec).astype(BF16), xdt, preferred_element_type=F32)
            y = jnp.where(head_of_lane == j, yj, y)
        hprev = h_ref[...]
        y = y + jnp.dot(cmb, hprev.astype(BF16), preferred_element_type=F32) * _expand_heads(jnp.exp(cum))
        xw = (xs * _expand_heads(dt * jnp.exp(total - cum))).astype(BF16)
        st = lax.dot_general(bmb, xw, (((0,), (0,)), ((), ())), preferred_element_type=F32)
        h_ref[...] = hprev * _expand_heads(jnp.exp(total)) + st
        if direction == 0:
            y_ref[pl.ds(t0, L), :] = y
        else:
            y = y_ref[pl.ds(t0, L), :] + y + dsk_ref[...] * xs
            y = y * _silu(z_ref[0, pl.ds(t0, L), :])
            o_ref[0, pl.ds(t0, L), :] = _rms(y, gn_ref[...]).astype(o_ref.dtype)

    h_ref[...] = jnp.zeros_like(h_ref)

    def fwd(c, carry):
        for u in range(SSD_UNROLL):
            chunk(c * SSD_UNROLL + u, 0)
        return carry
    lax.fori_loop(0, nc // SSD_UNROLL, fwd, 0)
    h_ref[...] = jnp.zeros_like(h_ref)

    def bwd(i, carry):
        for u in range(SSD_UNROLL):
            chunk(nc - 1 - (i * SSD_UNROLL + u), 1)
        return carry
    lax.fori_loop(0, nc // SSD_UNROLL, bwd, 0)


def _ssd(xbc3, z3, dt4, dtt4, conv_w, conv_b, dtb, alog, dsk, gn):
    b, s, _ = xbc3.shape
    g_off_b = SSD_INNER // SSD_STATE
    g_off_c = g_off_b + SSD_GROUPS
    xcol = pl.BlockSpec((1, s, GROUP_COLS), lambda bi, g: (bi, 0, g))
    bcol = pl.BlockSpec((1, s, SSD_STATE), lambda bi, g: (bi, 0, g_off_b + g))
    ccol = pl.BlockSpec((1, s, SSD_STATE), lambda bi, g: (bi, 0, g_off_c + g))
    return pl.pallas_call(
        functools.partial(_ssd_kernel, s=s),
        grid=(b, SSD_GROUPS),
        in_specs=[
            xcol, bcol, ccol,
            pl.BlockSpec((1, s, GROUP_COLS), lambda bi, g: (bi, 0, g)),
            pl.BlockSpec((1, 1, s, 2 * HEADS_PER_GROUP), lambda bi, g: (g, bi, 0, 0)),
            pl.BlockSpec((1, 1, 2 * HEADS_PER_GROUP, s), lambda bi, g: (g, bi, 0, 0)),
            pl.BlockSpec((SSD_CONV, GROUP_COLS), lambda bi, g: (0, g)),
            pl.BlockSpec((SSD_CONV, SSD_STATE), lambda bi, g: (0, g_off_b + g)),
            pl.BlockSpec((SSD_CONV, SSD_STATE), lambda bi, g: (0, g_off_c + g)),
            pl.BlockSpec((1, GROUP_COLS), lambda bi, g: (0, g)),
            pl.BlockSpec((1, SSD_STATE), lambda bi, g: (0, g_off_b + g)),
            pl.BlockSpec((1, SSD_STATE), lambda bi, g: (0, g_off_c + g)),
            pl.BlockSpec((1, 1, 2 * HEADS_PER_GROUP), lambda bi, g: (g, 0, 0)),
            pl.BlockSpec((1, 1, 2 * HEADS_PER_GROUP), lambda bi, g: (g, 0, 0)),
            pl.BlockSpec((1, 2 * HEADS_PER_GROUP, 1), lambda bi, g: (g, 0, 0)),
            pl.BlockSpec((1, 2 * HEADS_PER_GROUP, 1), lambda bi, g: (g, 0, 0)),
            pl.BlockSpec((1, GROUP_COLS), lambda bi, g: (0, g)),
            pl.BlockSpec((1, GROUP_COLS), lambda bi, g: (0, g)),
        ],
        out_specs=pl.BlockSpec((1, s, GROUP_COLS), lambda bi, g: (bi, 0, g)),
        out_shape=jax.ShapeDtypeStruct((b, s, SSD_INNER), BF16),
        scratch_shapes=[
            pltpu.VMEM((s, GROUP_COLS), F32), pltpu.VMEM((s, SSD_STATE), F32), pltpu.VMEM((s, SSD_STATE), F32),
            pltpu.VMEM((s, GROUP_COLS), F32), pltpu.VMEM((SSD_STATE, GROUP_COLS), F32),
        ],
        compiler_params=_cparams(("parallel", "parallel")),
        name="ssd",
    )(xbc3, xbc3, xbc3, z3, dt4, dtt4, conv_w, conv_w, conv_w, conv_b, conv_b, conv_b, dtb, alog,
      dtb.reshape(SSD_GROUPS, -1, 1), alog.reshape(SSD_GROUPS, -1, 1), dsk, gn)


def _out_proj_kernel(x_ref, attn_ref, ssd_ref, wa_ref, ws_ref, g_ref, wr_ref, x1_ref, hn_ref, aff_ref):
    x1 = x_ref[...] + jnp.dot(attn_ref[...], wa_ref[...], preferred_element_type=F32) \
        + jnp.dot(ssd_ref[...], ws_ref[...], preferred_element_type=F32)
    x1_ref[...] = x1
    hn = _rms(x1, g_ref[...])
    hn_ref[...] = hn.astype(BF16)
    logits = lax.dot_general(wr_ref[...], hn, (((1,), (1,)), ((), ())), preferred_element_type=F32,
                             precision=lax.Precision.HIGHEST)
    e = jnp.exp(logits - jnp.max(logits, axis=0, keepdims=True))
    aff_ref[...] = e / jnp.sum(e, axis=0, keepdims=True)


def _out_proj(x2, attn2, ssd2, wa, ws, g_ffn, wr_t):
    t = x2.shape[0]
    tm = TM_PROJ
    full = lambda a: pl.BlockSpec(a.shape, lambda i: (0,) * a.ndim)
    return pl.pallas_call(
        _out_proj_kernel,
        grid=(t // tm,),
        in_specs=[pl.BlockSpec((tm, D_MODEL), lambda i: (i, 0)),
                  pl.BlockSpec((tm, ATTN_WIDTH), lambda i: (i, 0)),
                  pl.BlockSpec((tm, SSD_INNER), lambda i: (i, 0)),
                  full(wa), full(ws), full(g_ffn), full(wr_t)],
        out_specs=[pl.BlockSpec((tm, D_MODEL), lambda i: (i, 0)),
                   pl.BlockSpec((tm, D_MODEL), lambda i: (i, 0)),
                   pl.BlockSpec((N_EXPERTS, tm), lambda i: (0, i))],
        out_shape=[jax.ShapeDtypeStruct((t, D_MODEL), F32),
                   jax.ShapeDtypeStruct((t, D_MODEL), BF16),
                   jax.ShapeDtypeStruct((N_EXPERTS, t), F32)],
        compiler_params=_cparams(("parallel",)),
        name="out_proj",
    )(x2, attn2, ssd2, wa, ws, g_ffn, wr_t)


def _select_kernel(aff_ref, mask_ref, gate_ref, bits_ref, *, t, cap):
    nchunk = t // SEL_CHUNK
    lane = lax.broadcasted_iota(I32, (N_EXPERTS, SEL_CHUNK), 1)

    def to_bits(i, carry):
        sl = pl.ds(pl.multiple_of(i * SEL_CHUNK, SEL_CHUNK), SEL_CHUNK)
        bits_ref[:, sl] = lax.bitcast_convert_type(aff_ref[:, sl], I32)
        return carry
    lax.fori_loop(0, nchunk, to_bits, 0)

    def count(pred):
        def body(i, acc):
            sl = pl.ds(pl.multiple_of(i * SEL_CHUNK, SEL_CHUNK), SEL_CHUNK)
            return acc + pred(bits_ref[:, sl], lane + i * SEL_CHUNK).astype(F32)
        acc = lax.fori_loop(0, nchunk, body, jnp.zeros((N_EXPERTS, SEL_CHUNK), F32))
        return jnp.sum(acc, axis=1, keepdims=True)

    thr = jnp.zeros((N_EXPERTS, 1), I32)
    for bit in range(30, -1, -1):
        cand = thr | (1 << bit)
        cnt = count(lambda b, idx, cand=cand: b >= cand)
        thr = jnp.where(cnt >= cap, cand, thr)
    need = cap - count(lambda b, idx: b > thr)
    bound = jnp.zeros((N_EXPERTS, 1), I32)
    for bit in range(t.bit_length() - 1, -1, -1):
        cand = bound | (1 << bit)
        cnt = count(lambda b, idx, cand=cand: (b == thr) & (idx < cand))
        bound = jnp.where(cnt <= need, cand, bound)

    def emit(i, carry):
        sl = pl.ds(pl.multiple_of(i * SEL_CHUNK, SEL_CHUNK), SEL_CHUNK)
        b = bits_ref[:, sl]
        sel = (b > thr) | ((b == thr) & ((lane + i * SEL_CHUNK) < bound))
        mask_ref[:, sl] = sel.astype(F32)
        gate_ref[:, sl] = jnp.where(sel, aff_ref[:, sl], 0.0)
        return carry
    lax.fori_loop(0, nchunk, emit, 0)


def _select(aff_t, cap):
    t = aff_t.shape[1]
    full = pl.BlockSpec((N_EXPERTS, t), lambda i: (0, 0))
    return pl.pallas_call(
        functools.partial(_select_kernel, t=t, cap=cap),
        grid=(1,),
        in_specs=[full],
        out_specs=[full, full],
        out_shape=[jax.ShapeDtypeStruct((N_EXPERTS, t), F32)] * 2,
        scratch_shapes=[pltpu.VMEM((N_EXPERTS, t), I32)],
        compiler_params=_cparams(("arbitrary",)),
        name="select",
    )(aff_t)


def _dispatch_kernel(base_ref, cnt_ref, hn_ref, mask_ref, xs_ref, pos_ref,
                     rank_ref, onehot_ref, buf_ref, xbuf_ref, tail_ref, sem_ref, xsem_ref):
    i = pl.program_id(0)
    last = pl.num_programs(0) - 1
    tb = hn_ref.shape[0]
    slot = i % 2

    @pl.when(i == 0)
    def _():
        tail_ref[...] = jnp.zeros_like(tail_ref)
        xbuf_ref[...] = jnp.zeros_like(xbuf_ref)
        pad = [pltpu.make_async_copy(xbuf_ref, xs_ref.at[e, pl.ds(xs_ref.shape[1] - STRIP, STRIP), :], xsem_ref.at[0])
               for e in range(N_EXPERTS)]
        for cp in pad:
            cp.start()
        for cp in pad:
            cp.wait()

    r_i = lax.broadcasted_iota(I32, (tb, tb), 0)
    c_i = lax.broadcasted_iota(I32, (tb, tb), 1)
    incl = jnp.dot(mask_ref[...].astype(BF16), (r_i <= c_i).astype(BF16), preferred_element_type=F32)
    rank_ref[...] = incl - mask_ref[...]
    strip_row = lax.broadcasted_iota(I32, (STRIP, tb), 0).astype(F32)

    def onehot(e, k):
        off = (base_ref[i * N_EXPERTS + e] % BF16_ROWS).astype(F32)
        target = rank_ref[pl.ds(e, 1), :] + off
        sel = mask_ref[pl.ds(e, 1), :] > 0.0
        return (((strip_row + k * STRIP) == target) & sel).astype(BF16)

    def build(e, carry):
        pos_ref[pl.ds(e, 1), :] = rank_ref[pl.ds(e, 1), :] + base_ref[i * N_EXPERTS + e].astype(F32)
        onehot_ref[pl.ds(pl.multiple_of(e * STRIP, STRIP), STRIP), :] = onehot(e, 0)
        return carry
    lax.fori_loop(0, N_EXPERTS, build, 0)
    buf_ref[slot] = jnp.dot(onehot_ref[...], hn_ref[...], preferred_element_type=F32).astype(BF16)

    def strip_copy(s_, e, start):
        return pltpu.make_async_copy(buf_ref.at[s_, pl.ds(e * STRIP, STRIP), :],
                                     xs_ref.at[e, pl.ds(start, STRIP), :], sem_ref.at[s_, e])

    @pl.when(i > 0)
    def _():
        for e in range(N_EXPERTS):
            strip_copy(1 - slot, e, 0).wait()

    def finish(e, carry):
        base = base_ref[i * N_EXPERTS + e]
        off = base % BF16_ROWS
        end = off + cnt_ref[i * N_EXPERTS + e]
        row0 = pl.multiple_of(e * STRIP, STRIP)
        head = pl.ds(row0, BF16_ROWS)
        buf_ref[slot, head, :] = buf_ref[slot, head, :] + tail_ref[e]
        group = end // BF16_ROWS * BF16_ROWS
        has_tail = end % BF16_ROWS != 0

        @pl.when(jnp.logical_not(has_tail))
        def _():
            tail_ref[e] = jnp.zeros((BF16_ROWS, D_MODEL), BF16)

        @pl.when(has_tail & (group < STRIP))
        def _():
            tail_ref[e] = buf_ref[slot, pl.ds(pl.multiple_of(row0 + group, BF16_ROWS), BF16_ROWS), :]

        def extra(k, carry):
            xbuf_ref[...] = jnp.dot(onehot(e, k), hn_ref[...], preferred_element_type=F32).astype(BF16)

            @pl.when(has_tail & (group // STRIP == k))
            def _():
                tail_ref[e] = xbuf_ref[pl.ds(pl.multiple_of(group - k * STRIP, BF16_ROWS), BF16_ROWS), :]
            start = pl.multiple_of(base - off + k * STRIP, BF16_ROWS)
            cp = pltpu.make_async_copy(xbuf_ref, xs_ref.at[e, pl.ds(start, STRIP), :], xsem_ref.at[0])
            cp.start()
            cp.wait()
            return carry
        lax.fori_loop(1, jnp.maximum((end + STRIP - 1) // STRIP, 1), extra, 0)
        return carry
    lax.fori_loop(0, N_EXPERTS, finish, 0)

    for e in range(N_EXPERTS):
        base = base_ref[i * N_EXPERTS + e]
        strip_copy(slot, e, pl.multiple_of(base - base % BF16_ROWS, BF16_ROWS)).start()

    @pl.when(i == last)
    def _():
        for e in range(N_EXPERTS):
            strip_copy(slot, e, 0).wait()


def _dispatch(hn, mask_t, base, cnt, cap):
    t = hn.shape[0]
    tb = TB_ROUTE
    grid_spec = pltpu.PrefetchScalarGridSpec(
        num_scalar_prefetch=2,
        grid=(t // tb,),
        in_specs=[pl.BlockSpec((tb, D_MODEL), lambda i, *_: (i, 0)),
                  pl.BlockSpec((N_EXPERTS, tb), lambda i, *_: (0, i))],
        out_specs=[pl.BlockSpec(memory_space=pl.ANY),
                   pl.BlockSpec((N_EXPERTS, tb), lambda i, *_: (0, i))],
        scratch_shapes=[pltpu.VMEM((N_EXPERTS, tb), F32),
                        pltpu.VMEM((N_EXPERTS * STRIP, tb), BF16),
                        pltpu.VMEM((2, N_EXPERTS * STRIP, D_MODEL), BF16),
                        pltpu.VMEM((STRIP, D_MODEL), BF16),
                        pltpu.VMEM((N_EXPERTS, BF16_ROWS, D_MODEL), BF16),
                        pltpu.SemaphoreType.DMA((2, N_EXPERTS)), pltpu.SemaphoreType.DMA((1,))],
    )
    return pl.pallas_call(
        _dispatch_kernel,
        grid_spec=grid_spec,
        out_shape=[jax.ShapeDtypeStruct((N_EXPERTS, cap + STRIP, D_MODEL), BF16),
                   jax.ShapeDtypeStruct((N_EXPERTS, t), F32)],
        compiler_params=_cparams(("arbitrary",)),
        name="dispatch",
    )(base, cnt, hn, mask_t)


def _ffn_kernel(xs_ref, wg_ref, wu_ref, wd_ref, ye_ref):
    pad_tile = pl.program_id(1) == pl.num_programs(1) - 1

    @pl.when(jnp.logical_not(pad_tile))
    def _():
        x = xs_ref[0]
        g = jnp.dot(x, wg_ref[0], preferred_element_type=F32)
        u = jnp.dot(x, wu_ref[0], preferred_element_type=F32)
        a = (_silu(g) * u).astype(BF16)
        ye_ref[0] = jnp.dot(a, wd_ref[0], preferred_element_type=F32).astype(ye_ref.dtype)

    @pl.when(pad_tile)
    def _():
        ye_ref[...] = jnp.zeros_like(ye_ref)


def _ffn(xs, wg, wu, wd, cap):
    ntile = cap // TM_FFN
    once = pl.Buffered(1)
    return pl.pallas_call(
        _ffn_kernel,
        grid=(N_EXPERTS, ntile + 1),
        in_specs=[pl.BlockSpec((1, TM_FFN, D_MODEL), lambda e, j: (e, jnp.minimum(j, ntile - 1), 0)),
                  pl.BlockSpec((1, D_MODEL, EXPERT_FF), lambda e, j: (e, 0, 0), pipeline_mode=once),
                  pl.BlockSpec((1, D_MODEL, EXPERT_FF), lambda e, j: (e, 0, 0), pipeline_mode=once),
                  pl.BlockSpec((1, EXPERT_FF, D_MODEL), lambda e, j: (e, 0, 0), pipeline_mode=once)],
        out_specs=pl.BlockSpec((1, TM_FFN, D_MODEL), lambda e, j: (e, j, 0)),
        out_shape=jax.ShapeDtypeStruct((N_EXPERTS, cap + TM_FFN, D_MODEL), BF16),
        compiler_params=_cparams(("parallel", "parallel")),
        name="expert_ffn",
    )(xs, wg, wu, wd)


def _combine_kernel(start_ref, nstrip_ref, x1_ref, pos_ref, gate_ref, pe_ref, ye_ref, wple_ref, wpg_ref,
                    gple_ref, gpg_ref, gfin_ref, o_ref, acc_ref, w_ref, buf_ref, sem_ref, xbuf_ref, xsem_ref):
    i = pl.program_id(0)
    ntile = pl.num_programs(0)
    tb = x1_ref.shape[0]
    slot = i % 2
    lane = lax.broadcasted_iota(I32, (tb, LANES), 1)
    left = lane < STRIP
    lane_f = lane.astype(F32)

    def strip_copy(s_, tile, e):
        start = pl.multiple_of(start_ref[tile * N_EXPERTS + e], BF16_ROWS)
        return pltpu.make_async_copy(ye_ref.at[e, pl.ds(start, STRIP), :],
                                     buf_ref.at[s_, pl.ds(e * STRIP, STRIP), :], sem_ref.at[s_, e])

    @pl.when(i == 0)
    def _():
        for e in range(N_EXPERTS):
            strip_copy(0, 0, e).start()

    @pl.when(i + 1 < ntile)
    def _():
        for e in range(N_EXPERTS):
            strip_copy(1 - slot, i + 1, e).start()

    def rel_gate(e, width=LANES):
        rel = pos_ref[:, e:e + 1] - start_ref[i * N_EXPERTS + e].astype(F32)
        return jnp.broadcast_to(rel, (tb, width)), jnp.broadcast_to(gate_ref[:, e:e + 1], (tb, width))

    assert 2 * STRIP == LANES
    for pair in range(N_EXPERTS // 2):
        rel0, g0 = rel_gate(2 * pair)
        rel1, g1 = rel_gate(2 * pair + 1)
        target = jnp.where(left, rel0, rel1 + STRIP)
        gate = jnp.where(left, g0, g1)
        w_ref[:, pair * LANES:(pair + 1) * LANES] = jnp.where(lane_f == target, gate, 0.0).astype(BF16)

    for e in range(N_EXPERTS):
        strip_copy(slot, i, e).wait()
    acc_ref[...] = x1_ref[...] + jnp.dot(w_ref[...], buf_ref[slot], preferred_element_type=F32)

    for e in range(N_EXPERTS):
        def extra(k, carry, e=e):
            start = pl.multiple_of(start_ref[i * N_EXPERTS + e] + k * STRIP, BF16_ROWS)
            cp = pltpu.make_async_copy(ye_ref.at[e, pl.ds(start, STRIP), :], xbuf_ref, xsem_ref.at[0])
            cp.start()
            rel, gate = rel_gate(e, STRIP)
            strip_lane = lax.broadcasted_iota(I32, (tb, STRIP), 1) + k * STRIP
            w = jnp.where(strip_lane.astype(F32) == rel, gate, 0.0).astype(BF16)
            cp.wait()
            acc_ref[...] += jnp.dot(w, xbuf_ref[...], preferred_element_type=F32)
            return carry
        lax.fori_loop(1, jnp.maximum(nstrip_ref[i * N_EXPERTS + e], 1), extra, 0)

    x2 = acc_ref[...]
    ple = _rms(jnp.dot(pe_ref[...].astype(BF16), wple_ref[...], preferred_element_type=F32), gple_ref[...])
    pg = _sigmoid(jnp.dot(_rms(x2, gpg_ref[...]).astype(BF16), wpg_ref[...], preferred_element_type=F32))
    o_ref[...] = _rms(x2 + pg * ple, gfin_ref[...])


def _combine(start, nstrip, x1, pos, gate, pe2, ye, wple, wpg, g_ple, g_pg, g_final):
    t = x1.shape[0]
    tb = TB_ROUTE
    full = lambda a: pl.BlockSpec(a.shape, lambda i, *_: (0,) * a.ndim)
    grid_spec = pltpu.PrefetchScalarGridSpec(
        num_scalar_prefetch=2,
        grid=(t // tb,),
        in_specs=[pl.BlockSpec((tb, D_MODEL), lambda i, *_: (i, 0)),
                  pl.BlockSpec((tb, N_EXPERTS), lambda i, *_: (i, 0)),
                  pl.BlockSpec((tb, N_EXPERTS), lambda i, *_: (i, 0)),
                  pl.BlockSpec((tb, PLE_DIM), lambda i, *_: (i, 0)),
                  pl.BlockSpec(memory_space=pl.ANY),
                  full(wple), full(wpg), full(g_ple), full(g_pg), full(g_final)],
        out_specs=pl.BlockSpec((tb, D_MODEL), lambda i, *_: (i, 0)),
        scratch_shapes=[pltpu.VMEM((tb, D_MODEL), F32),
                        pltpu.VMEM((tb, N_EXPERTS * STRIP), BF16),
                        pltpu.VMEM((2, N_EXPERTS * STRIP, D_MODEL), BF16), pltpu.SemaphoreType.DMA((2, N_EXPERTS)),
                        pltpu.VMEM((STRIP, D_MODEL), BF16), pltpu.SemaphoreType.DMA((1,))],
    )
    return pl.pallas_call(
        _combine_kernel,
        grid_spec=grid_spec,
        out_shape=jax.ShapeDtypeStruct((t, D_MODEL), F32),
        compiler_params=_cparams(("arbitrary",)),
        name="combine",
    )(start, nstrip, x1, pos, gate, pe2, ye, wple, wpg, g_ple, g_pg, g_final)


def _moe(x1, hn, aff_t, pe2, w):
    t = x1.shape[0]
    cap = max(1, CAPACITY_FACTOR * t // N_EXPERTS)
    assert cap % TM_FFN == 0 and t % TB_ROUTE == 0
    mask_t, gate_t = _select(aff_t, cap)
    nt = t // TB_ROUTE
    cnt = jnp.sum(mask_t.reshape(N_EXPERTS, nt, TB_ROUTE), axis=2).astype(I32).T
    base = jnp.cumsum(cnt, axis=0) - cnt
    xs, pos_t = _dispatch(hn, mask_t, base.reshape(-1), cnt.reshape(-1), cap)
    ye = _ffn(xs, w["wg"], w["wu"], w["wd"], cap)
    start = base // BF16_ROWS * BF16_ROWS
    nstrip = jnp.where(cnt > 0, (base - start + cnt + STRIP - 1) // STRIP, 0)
    return _combine(start.reshape(-1), nstrip.reshape(-1), x1, pos_t.T, gate_t.T, pe2, ye,
                    w["wple"], w["wpg"], w["g_ple"], w["g_pg"], w["g_final"])


def _trunk(x, pe, w):
    b, s, _ = x.shape
    t = b * s
    x2 = x.reshape(t, D_MODEL)
    qkv, z, xbc, dt, dtt = _in_proj(x2, w["g_mix"], w["wqkv"], w["wz"], w["wxbc"], w["wdt"], w["wdtt"])
    attn = _attention(qkv.reshape(b, s, 3 * ATTN_WIDTH), w["bias_tabs"])
    ssd = _ssd(xbc.reshape(b, s, CONV_DIM), z.reshape(b, s, SSD_INNER),
               dt.reshape(SSD_GROUPS, b, s, 2 * HEADS_PER_GROUP),
               dtt.reshape(SSD_GROUPS, 2 * HEADS_PER_GROUP, b, s).transpose(0, 2, 1, 3),
               w["conv_w"], w["conv_b"], w["dtb"], w["alog"], w["dsk"], w["gn"])
    x1, hn, aff_t = _out_proj(x2, attn.reshape(t, ATTN_WIDTH), ssd.reshape(t, SSD_INNER),
                              w["wo_a"], w["wo_s"], w["g_ffn"], w["wr_t"])
    y = _moe(x1, hn, aff_t, pe.reshape(t, PLE_DIM), w)
    return y.reshape(b, s, D_MODEL)


def _group_heads(a):
    parts = [jnp.concatenate([a[..., 0, g * HEADS_PER_GROUP:(g + 1) * HEADS_PER_GROUP],
                              a[..., 1, g * HEADS_PER_GROUP:(g + 1) * HEADS_PER_GROUP]], axis=-1)
             for g in range(SSD_GROUPS)]
    return jnp.stack(parts)


def _prepare(rel_bias, g_mix, w_in, conv_w, conv_b, dt_bias, a_log, d_skip, g_ssd, w_out, g_ffn, w_router,
             w_gate, w_up, w_down, g_pg, w_pg, w_ple, g_ple, g_final):
    row = lambda v: v.reshape(1, -1).astype(F32)
    c0 = 3 * ATTN_WIDTH
    c1 = c0 + SSD_INNER
    c2 = c1 + CONV_DIM
    w_dt = w_in[:, c2:].reshape(D_MODEL, 2, SSD_HEADS)
    return dict(
        g_mix=row(g_mix), wqkv=w_in[:, :c0].astype(BF16), wz=w_in[:, c0:c1].astype(BF16),
        wxbc=w_in[:, c1:c2].astype(BF16), wdt=_group_heads(w_dt).astype(BF16),
        wdtt=_group_heads(w_dt).astype(BF16).transpose(0, 2, 1),
        bias_tabs=_attn_bias_tables(rel_bias),
        conv_w=conv_w.astype(F32), conv_b=row(conv_b),
        dtb=_group_heads(dt_bias.astype(F32)).reshape(SSD_GROUPS, 1, 2 * HEADS_PER_GROUP),
        alog=_group_heads(a_log.astype(F32)).reshape(SSD_GROUPS, 1, 2 * HEADS_PER_GROUP),
        dsk=row(jnp.repeat(d_skip, SSD_HEAD_DIM)), gn=row(g_ssd),
        wo_a=w_out[:ATTN_WIDTH].astype(BF16), wo_s=w_out[ATTN_WIDTH:].astype(BF16),
        g_ffn=row(g_ffn), wr_t=w_router.T.astype(F32),
        wg=w_gate.astype(BF16), wu=w_up.astype(BF16), wd=w_down.astype(BF16),
        g_pg=row(g_pg), wpg=w_pg.astype(BF16), wple=w_ple.astype(BF16), g_ple=row(g_ple), g_final=row(g_final),
    )


def kernel(x_prompt, x_sample, p_prompt, p_sample, rel_bias, g_mix, w_in, conv_w, conv_b, dt_bias, a_log, d_skip,
           g_ssd, w_out, g_ffn, w_router, w_gate, w_up, w_down, g_pg, w_pg, w_ple, g_ple, g_final):
    assert g_mix.shape[0] == 1, "single-layer trunk"
    w = _prepare(rel_bias, g_mix[0], w_in[0], conv_w[0], conv_b[0], dt_bias[0], a_log[0], d_skip[0], g_ssd[0],
                 w_out[0], g_ffn[0], w_router[0], w_gate[0], w_up[0], w_down[0], g_pg[0], w_pg[0], w_ple[0],
                 g_ple[0], g_final)
    return _trunk(x_prompt, p_prompt[0], w), _trunk(x_sample, p_sample[0], w)
```

```python
import functools
import math

import jax
import jax.numpy as jnp
import numpy as np
from jax import lax
from jax.experimental import pallas as pl
from jax.experimental.pallas import tpu as pltpu

F32 = jnp.float32
BF16 = jnp.bfloat16
I32 = jnp.int32

D_MODEL = 1024
PLE_DIM = 256
N_ATTN_HEADS = 8
ATTN_HEAD_DIM = 64
ATTN_WIDTH = N_ATTN_HEADS * ATTN_HEAD_DIM
DILATED_PATTERNS = ((128, 1), (512, 4), (2048, 16))
REL_BUCKETS = 32
REL_MAX_DISTANCE = 1024
SSD_HEADS = 8
SSD_HEAD_DIM = 64
SSD_INNER = SSD_HEADS * SSD_HEAD_DIM
SSD_GROUPS = 2
SSD_STATE = 128
SSD_CONV = 5
SSD_CHUNK = 128
CONV_DIM = SSD_INNER + 2 * SSD_GROUPS * SSD_STATE
N_EXPERTS = 16
EXPERT_FF = 2816
CAPACITY_FACTOR = 2
EPS = 1e-6

LANES = 128
SUBLANES = 8
BF16_ROWS = 16
VMEM_LIMIT = 56 * 1024 * 1024
NEG = -1e30

HEADS_PER_GROUP = SSD_HEADS // SSD_GROUPS
GROUP_COLS = SSD_INNER // SSD_GROUPS
ATT_HALF = 64
ATT_BQ = 128
ATT_KW = 256
TM_PROJ = 512
SSD_UNROLL = 4
ATT_UNROLL = 16
TB_ROUTE = 256
STRIP = 64
TM_FFN = 512
SEL_CHUNK = 1024


def _cparams(sem):
    return pltpu.CompilerParams(dimension_semantics=sem, vmem_limit_bytes=VMEM_LIMIT)


def _rms(x, g):
    return x * lax.rsqrt(jnp.mean(x * x, axis=-1, keepdims=True) + EPS) * g


def _sigmoid(x):
    return 0.5 * jnp.tanh(0.5 * x) + 0.5


def _silu(x):
    return x * _sigmoid(x)


def _in_proj_kernel(x_ref, g_ref, wqkv_ref, wz_ref, wxbc_ref, wdt_ref, qkv_ref, z_ref, xbc_ref, dt_ref):
    h = _rms(x_ref[...], g_ref[...]).astype(BF16)
    qkv_ref[...] = jnp.dot(h, wqkv_ref[...], preferred_element_type=F32)
    z_ref[...] = jnp.dot(h, wz_ref[...], preferred_element_type=F32)
    xbc_ref[...] = jnp.dot(h, wxbc_ref[...], preferred_element_type=F32)
    for g in range(SSD_GROUPS):
        dt_ref[g] = jnp.dot(h, wdt_ref[g], preferred_element_type=F32)


def _in_proj(x2, g_mix, wqkv, wz, wxbc, wdt):
    t = x2.shape[0]
    tm = TM_PROJ
    full = lambda a: pl.BlockSpec(a.shape, lambda i: (0,) * a.ndim)
    return pl.pallas_call(
        _in_proj_kernel,
        grid=(t // tm,),
        in_specs=[pl.BlockSpec((tm, D_MODEL), lambda i: (i, 0)), full(g_mix), full(wqkv), full(wz), full(wxbc), full(wdt)],
        out_specs=[
            pl.BlockSpec((tm, 3 * ATTN_WIDTH), lambda i: (i, 0)),
            pl.BlockSpec((tm, SSD_INNER), lambda i: (i, 0)),
            pl.BlockSpec((tm, CONV_DIM), lambda i: (i, 0)),
            pl.BlockSpec((SSD_GROUPS, tm, 2 * HEADS_PER_GROUP), lambda i: (0, i, 0)),
        ],
        out_shape=[
            jax.ShapeDtypeStruct((t, 3 * ATTN_WIDTH), F32),
            jax.ShapeDtypeStruct((t, SSD_INNER), F32),
            jax.ShapeDtypeStruct((t, CONV_DIM), F32),
            jax.ShapeDtypeStruct((SSD_GROUPS, t, 2 * HEADS_PER_GROUP), F32),
        ],
        compiler_params=_cparams(("parallel",)),
        name="in_proj",
    )(x2, g_mix, wqkv, wz, wxbc, wdt)


def _t5_bucket(rel):
    half = REL_BUCKETS // 2
    max_exact = half // 2
    n = np.abs(rel)
    large = max_exact + (np.log(np.maximum(n, 1) / max_exact) / math.log(REL_MAX_DISTANCE / max_exact) * (half - max_exact)).astype(np.int32)
    large = np.minimum(large, half - 1)
    return (np.where(rel > 0, half, 0) + np.where(n < max_exact, n, large)).astype(np.int32)


_ATT_WINDOW_OFFSETS = (0, -ATT_HALF, -(ATT_KW - ATT_BQ))


def _attn_bias_tables(rel_bias):
    buckets = []
    for window, dilation in DILATED_PATTERNS:
        assert window // (2 * dilation) == ATT_HALF
        for off in _ATT_WINDOW_OFFSETS:
            m = np.arange(ATT_KW)[None, :] + off - np.arange(ATT_BQ)[:, None]
            buckets.append(np.where(np.abs(m) <= ATT_HALF, _t5_bucket(np.clip(m, -ATT_HALF, ATT_HALF) * dilation), -1))
    bucket = jnp.asarray(np.stack(buckets).reshape(len(DILATED_PATTERNS), len(_ATT_WINDOW_OFFSETS), 1, ATT_BQ, ATT_KW))
    tab = jnp.full(bucket.shape[:2] + (N_ATTN_HEADS, ATT_BQ, ATT_KW), NEG, F32)
    for bkt in range(REL_BUCKETS):
        tab = jnp.where(bucket == bkt, rel_bias[bkt].astype(F32)[None, None, :, None, None], tab)
    return tab


def _attn_kernel(q_ref, k_ref, v_ref, bias_ref, o_ref, acc_ref, m_ref, l_ref, *, s):
    lane = lax.broadcasted_iota(I32, (ATT_BQ, LANES), 1)
    first_head = lane < ATTN_HEAD_DIM
    scale = ATTN_HEAD_DIM ** -0.5
    nblk = s // ATT_BQ

    def rows(start, size, d):
        return pl.ds(start, size) if d == 1 else pl.ds(start, size, stride=d)

    def block(p, d, f):
        n = s // d
        nq = n // ATT_BQ
        kw = min(ATT_KW, n)
        r = f // nq
        q0 = (f % nq) * ATT_BQ
        ks = jnp.clip(q0 - ATT_HALF, 0, n - kw)
        kind = (q0 - ks) // ATT_HALF
        q_rows = rows(r + d * q0, ATT_BQ, d)
        k_rows = rows(r + d * ks, kw, d)
        qb = q_ref[0, q_rows, :] * scale
        kb = k_ref[0, k_rows, :].astype(BF16)
        vb = v_ref[0, k_rows, :].astype(BF16)
        outs, ms, ls = [], [], []
        for hh in range(2):
            sel = first_head if hh == 0 else jnp.logical_not(first_head)
            qm = jnp.where(sel, qb, 0.0).astype(BF16)
            sc = lax.dot_general(qm, kb, (((1,), (1,)), ((), ())), preferred_element_type=F32)
            sc = sc + bias_ref[p, kind, hh, :, :kw]
            mx = jnp.max(sc, axis=-1, keepdims=True)
            pe = jnp.exp(sc - mx)
            ls.append(jnp.sum(pe, axis=-1, keepdims=True))
            ms.append(mx)
            outs.append(jnp.dot(pe.astype(BF16), vb, preferred_element_type=F32))
        acc_ref[p, q_rows, :] = jnp.where(first_head, outs[0], outs[1])
        m_ref[p, q_rows, :] = jnp.where(first_head, ms[0], ms[1])
        l_ref[p, q_rows, :] = jnp.where(first_head, ls[0], ls[1])

    for p, (_, d) in enumerate(DILATED_PATTERNS):
        def body(it, carry, p=p, d=d):
            for u in range(ATT_UNROLL):
                block(p, d, it * ATT_UNROLL + u)
            return carry
        lax.fori_loop(0, nblk // ATT_UNROLL, body, 0)

    def finish(i, carry):
        sl = pl.ds(pl.multiple_of(i * ATT_BQ, ATT_BQ), ATT_BQ)
        ms = [m_ref[p, sl, :] for p in range(len(DILATED_PATTERNS))]
        mx = functools.reduce(jnp.maximum, ms)
        ws = [jnp.exp(m - mx) for m in ms]
        num = sum(w * acc_ref[p, sl, :] for p, w in enumerate(ws))
        den = sum(w * l_ref[p, sl, :] for p, w in enumerate(ws))
        o_ref[0, sl, :] = (num / den).astype(o_ref.dtype)
        return carry
    lax.fori_loop(0, nblk, finish, 0)


def _attention(qkv3, bias_tabs):
    b, s, _ = qkv3.shape
    assert s % (16 * ATT_BQ) == 0
    npairs = N_ATTN_HEADS // 2
    col = lambda base: pl.BlockSpec((1, s, LANES), lambda bi, hp: (bi, 0, base + hp))
    return pl.pallas_call(
        functools.partial(_attn_kernel, s=s),
        grid=(b, npairs),
        in_specs=[col(0), col(npairs), col(2 * npairs),
                  pl.BlockSpec((len(DILATED_PATTERNS), len(_ATT_WINDOW_OFFSETS), 2, ATT_BQ, ATT_KW), lambda bi, hp: (0, 0, hp, 0, 0))],
        out_specs=pl.BlockSpec((1, s, LANES), lambda bi, hp: (bi, 0, hp)),
        out_shape=jax.ShapeDtypeStruct((b, s, ATTN_WIDTH), BF16),
        scratch_shapes=[pltpu.VMEM((len(DILATED_PATTERNS), s, LANES), F32)] * 3,
        compiler_params=_cparams(("parallel", "parallel")),
        name="attention",
    )(qkv3, qkv3, qkv3, bias_tabs)


def _expand_heads(v):
    rows = v.shape[0]
    head = lax.broadcasted_iota(I32, (rows, GROUP_COLS), 1) // SSD_HEAD_DIM
    out = jnp.broadcast_to(v[:, 0:1], (rows, GROUP_COLS))
    for j in range(1, HEADS_PER_GROUP):
        out = jnp.where(head == j, v[:, j:j + 1], out)
    return out


def _ssd_kernel(xs_ref, b_ref, c_ref, z_ref, dt_ref, dtt_ref, wx_ref, wb_ref, wc_ref, bx_ref, bb_ref, bc_ref,
                dtb_ref, alog_ref, dtbc_ref, alogc_ref, dsk_ref, gn_ref, o_ref, xc_ref, bcv_ref, ccv_ref, y_ref, h_ref, *, s):
    L = SSD_CHUNK
    nc = s // L
    row = lax.broadcasted_iota(I32, (L, L), 0)
    colm = lax.broadcasted_iota(I32, (L, L), 1)
    lower = row >= colm
    upper = row <= colm
    tril = lower.astype(F32)
    triu = upper.astype(F32)
    head_of_lane = lax.broadcasted_iota(I32, (L, GROUP_COLS), 1) // SSD_HEAD_DIM
    a_neg = -jnp.exp(alog_ref[0])
    dt_bias = dtb_ref[0]
    a_neg_col = -jnp.exp(alogc_ref[0])
    dt_bias_col = dtbc_ref[0]

    def conv(ref, w_ref, bias_ref, c, t0):
        main = ref[0, pl.ds(t0, L), :]
        p0 = pl.multiple_of(jnp.maximum(t0 - SUBLANES, 0), SUBLANES)
        n0 = pl.multiple_of(jnp.minimum(t0 + L, s - SUBLANES), SUBLANES)
        prev = ref[0, pl.ds(p0, SUBLANES), :] * jnp.where(c > 0, 1.0, 0.0)
        nxt = ref[0, pl.ds(n0, SUBLANES), :] * jnp.where(c < nc - 1, 1.0, 0.0)
        ext = jnp.concatenate([prev, main, nxt], axis=0)
        acc = jnp.broadcast_to(bias_ref[...], main.shape)
        for k in range(SSD_CONV):
            lo = SUBLANES - SSD_CONV // 2 + k
            acc = acc + w_ref[k:k + 1, :] * ext[lo:lo + L, :]
        return _silu(acc)

    def chunk(c, direction):
        t0 = pl.multiple_of(c * L, L)
        if direction == 0:
            xs = conv(xs_ref, wx_ref, bx_ref, c, t0)
            bm = conv(b_ref, wb_ref, bb_ref, c, t0)
            cm = conv(c_ref, wc_ref, bc_ref, c, t0)
            xc_ref[pl.ds(t0, L), :] = xs
            bcv_ref[pl.ds(t0, L), :] = bm
            ccv_ref[pl.ds(t0, L), :] = cm
        else:
            xs = xc_ref[pl.ds(t0, L), :]
            bm = bcv_ref[pl.ds(t0, L), :]
            cm = ccv_ref[pl.ds(t0, L), :]
        x = dt_ref[0, 0, pl.ds(t0, L), :] + dt_bias
        dt_all = jnp.maximum(x, 0.0) + jnp.log1p(jnp.exp(-jnp.abs(x)))
        a_all = dt_all * a_neg
        lo = direction * HEADS_PER_GROUP
        dt = dt_all[:, lo:lo + HEADS_PER_GROUP]
        a = a_all[:, lo:lo + HEADS_PER_GROUP]
        cum = jnp.dot(tril if direction == 0 else triu, a, preferred_element_type=F32, precision=lax.Precision.HIGHEST)
        total = cum[L - 1:L, :] if direction == 0 else cum[0:1, :]
        xr = dtt_ref[0, 0, :, pl.ds(t0, L)] + dt_bias_col
        a_row = (jnp.maximum(xr, 0.0) + jnp.log1p(jnp.exp(-jnp.abs(xr)))) * a_neg_col
        cum_row = jnp.dot(a_row, triu if direction == 0 else tril, preferred_element_type=F32,
                          precision=lax.Precision.HIGHEST)
        mask = lower if direction == 0 else upper
        bmb = bm.astype(BF16)
        cmb = cm.astype(BF16)
        cb = lax.dot_general(cmb, bmb, (((1,), (1,)), ((), ())), preferred_element_type=F32)
        xdt = (xs * _expand_heads(dt)).astype(BF16)
        y = jnp.zeros((L, GROUP_COLS), F32)
        for j in range(HEADS_PER_GROUP):
            colb = jnp.broadcast_to(cum[:, j:j + 1], (L, L))
            rowb = jnp.broadcast_to(cum_row[lo + j:lo + j + 1, :], (L, L))
            seg = colb - rowb
            dec = jnp.exp(jnp.where(mask, seg, NEG))
            yj = jnp.dot((cb * dec).astype(BF16), xdt, preferred_element_type=F32)
            y = jnp.where(head_of_lane == j, yj, y)
        hprev = h_ref[...]
        y = y + jnp.dot(cmb, hprev.astype(BF16), preferred_element_type=F32) * _expand_heads(jnp.exp(cum))
        xw = (xs * _expand_heads(dt * jnp.exp(total - cum))).astype(BF16)
        st = lax.dot_general(bmb, xw, (((0,), (0,)), ((), ())), preferred_element_type=F32)
        h_ref[...] = hprev * _expand_heads(jnp.exp(total)) + st
        if direction == 0:
            y_ref[pl.ds(t0, L), :] = y
        else:
            y = y_ref[pl.ds(t0, L), :] + y + dsk_ref[...] * xs
            y = y * _silu(z_ref[0, pl.ds(t0, L), :])
            o_ref[0, pl.ds(t0, L), :] = _rms(y, gn_ref[...]).astype(o_ref.dtype)

    h_ref[...] = jnp.zeros_like(h_ref)

    def fwd(c, carry):
        for u in range(SSD_UNROLL):
            chunk(c * SSD_UNROLL + u, 0)
        return carry
    lax.fori_loop(0, nc // SSD_UNROLL, fwd, 0)
    h_ref[...] = jnp.zeros_like(h_ref)

    def bwd(i, carry):
        for u in range(SSD_UNROLL):
            chunk(nc - 1 - (i * SSD_UNROLL + u), 1)
        return carry
    lax.fori_loop(0, nc // SSD_UNROLL, bwd, 0)


def _ssd(xbc3, z3, dt4, dtt4, conv_w, conv_b, dtb, alog, dsk, gn):
    b, s, _ = xbc3.shape
    g_off_b = SSD_INNER // SSD_STATE
    g_off_c = g_off_b + SSD_GROUPS
    xcol = pl.BlockSpec((1, s, GROUP_COLS), lambda bi, g: (bi, 0, g))
    bcol = pl.BlockSpec((1, s, SSD_STATE), lambda bi, g: (bi, 0, g_off_b + g))
    ccol = pl.BlockSpec((1, s, SSD_STATE), lambda bi, g: (bi, 0, g_off_c + g))
    return pl.pallas_call(
        functools.partial(_ssd_kernel, s=s),
        grid=(b, SSD_GROUPS),
        in_specs=[
            xcol, bcol, ccol,
            pl.BlockSpec((1, s, GROUP_COLS), lambda bi, g: (bi, 0, g)),
            pl.BlockSpec((1, 1, s, 2 * HEADS_PER_GROUP), lambda bi, g: (g, bi, 0, 0)),
            pl.BlockSpec((1, 1, 2 * HEADS_PER_GROUP, s), lambda bi, g: (g, bi, 0, 0)),
            pl.BlockSpec((SSD_CONV, GROUP_COLS), lambda bi, g: (0, g)),
            pl.BlockSpec((SSD_CONV, SSD_STATE), lambda bi, g: (0, g_off_b + g)),
            pl.BlockSpec((SSD_CONV, SSD_STATE), lambda bi, g: (0, g_off_c + g)),
            pl.BlockSpec((1, GROUP_COLS), lambda bi, g: (0, g)),
            pl.BlockSpec((1, SSD_STATE), lambda bi, g: (0, g_off_b + g)),
            pl.BlockSpec((1, SSD_STATE), lambda bi, g: (0, g_off_c + g)),
            pl.BlockSpec((1, 1, 2 * HEADS_PER_GROUP), lambda bi, g: (g, 0, 0)),
            pl.BlockSpec((1, 1, 2 * HEADS_PER_GROUP), lambda bi, g: (g, 0, 0)),
            pl.BlockSpec((1, 2 * HEADS_PER_GROUP, 1), lambda bi, g: (g, 0, 0)),
            pl.BlockSpec((1, 2 * HEADS_PER_GROUP, 1), lambda bi, g: (g, 0, 0)),
            pl.BlockSpec((1, GROUP_COLS), lambda bi, g: (0, g)),
            pl.BlockSpec((1, GROUP_COLS), lambda bi, g: (0, g)),
        ],
        out_specs=pl.BlockSpec((1, s, GROUP_COLS), lambda bi, g: (bi, 0, g)),
        out_shape=jax.ShapeDtypeStruct((b, s, SSD_INNER), BF16),
        scratch_shapes=[
            pltpu.VMEM((s, GROUP_COLS), F32), pltpu.VMEM((s, SSD_STATE), F32), pltpu.VMEM((s, SSD_STATE), F32),
            pltpu.VMEM((s, GROUP_COLS), F32), pltpu.VMEM((SSD_STATE, GROUP_COLS), F32),
        ],
        compiler_params=_cparams(("parallel", "parallel")),
        name="ssd",
    )(xbc3, xbc3, xbc3, z3, dt4, dtt4, conv_w, conv_w, conv_w, conv_b, conv_b, conv_b, dtb, alog,
      dtb.reshape(SSD_GROUPS, -1, 1), alog.reshape(SSD_GROUPS, -1, 1), dsk, gn)


def _out_proj_kernel(x_ref, attn_ref, ssd_ref, wa_ref, ws_ref, g_ref, wr_ref, x1_ref, hn_ref, aff_ref):
    x1 = x_ref[...] + jnp.dot(attn_ref[...], wa_ref[...], preferred_element_type=F32) \
        + jnp.dot(ssd_ref[...], ws_ref[...], preferred_element_type=F32)
    x1_ref[...] = x1
    hn = _rms(x1, g_ref[...])
    hn_ref[...] = hn.astype(BF16)
    logits = lax.dot_general(wr_ref[...], hn, (((1,), (1,)), ((), ())), preferred_element_type=F32,
                             precision=lax.Precision.HIGHEST)
    e = jnp.exp(logits - jnp.max(logits, axis=0, keepdims=True))
    aff_ref[...] = e / jnp.sum(e, axis=0, keepdims=True)


def _out_proj(x2, attn2, ssd2, wa, ws, g_ffn, wr_t):
    t = x2.shape[0]
    tm = TM_PROJ
    full = lambda a: pl.BlockSpec(a.shape, lambda i: (0,) * a.ndim)
    return pl.pallas_call(
        _out_proj_kernel,
        grid=(t // tm,),
        in_specs=[pl.BlockSpec((tm, D_MODEL), lambda i: (i, 0)),
                  pl.BlockSpec((tm, ATTN_WIDTH), lambda i: (i, 0)),
                  pl.BlockSpec((tm, SSD_INNER), lambda i: (i, 0)),
                  full(wa), full(ws), full(g_ffn), full(wr_t)],
        out_specs=[pl.BlockSpec((tm, D_MODEL), lambda i: (i, 0)),
                   pl.BlockSpec((tm, D_MODEL), lambda i: (i, 0)),
                   pl.BlockSpec((N_EXPERTS, tm), lambda i: (0, i))],
        out_shape=[jax.ShapeDtypeStruct((t, D_MODEL), F32),
                   jax.ShapeDtypeStruct((t, D_MODEL), BF16),
                   jax.ShapeDtypeStruct((N_EXPERTS, t), F32)],
        compiler_params=_cparams(("parallel",)),
        name="out_proj",
    )(x2, attn2, ssd2, wa, ws, g_ffn, wr_t)


def _select_kernel(aff_ref, mask_ref, gate_ref, bits_ref, *, t, cap):
    nchunk = t // SEL_CHUNK
    lane = lax.broadcasted_iota(I32, (N_EXPERTS, SEL_CHUNK), 1)

    def to_bits(i, carry):
        sl = pl.ds(pl.multiple_of(i * SEL_CHUNK, SEL_CHUNK), SEL_CHUNK)
        bits_ref[:, sl] = lax.bitcast_convert_type(aff_ref[:, sl], I32)
        return carry
    lax.fori_loop(0, nchunk, to_bits, 0)

    def count(pred):
        def body(i, acc):
            sl = pl.ds(pl.multiple_of(i * SEL_CHUNK, SEL_CHUNK), SEL_CHUNK)
            return acc + pred(bits_ref[:, sl], lane + i * SEL_CHUNK).astype(F32)
        acc = lax.fori_loop(0, nchunk, body, jnp.zeros((N_EXPERTS, SEL_CHUNK), F32))
        return jnp.sum(acc, axis=1, keepdims=True)

    thr = jnp.zeros((N_EXPERTS, 1), I32)
    for bit in range(30, -1, -1):
        cand = thr | (1 << bit)
        cnt = count(lambda b, idx, cand=cand: b >= cand)
        thr = jnp.where(cnt >= cap, cand, thr)
    need = cap - count(lambda b, idx: b > thr)
    bound = jnp.zeros((N_EXPERTS, 1), I32)
    for bit in range(t.bit_length() - 1, -1, -1):
        cand = bound | (1 << bit)
        cnt = count(lambda b, idx, cand=cand: (b == thr) & (idx < cand))
        bound = jnp.where(cnt <= need, cand, bound)

    def emit(i, carry):
        sl = pl.ds(pl.multiple_of(i * SEL_CHUNK, SEL_CHUNK), SEL_CHUNK)
        b = bits_ref[:, sl]
        sel = (b > thr) | ((b == thr) & ((lane + i * SEL_CHUNK) < bound))
        mask_ref[:, sl] = sel.astype(F32)
        gate_ref[:, sl] = jnp.where(sel, aff_ref[:, sl], 0.0)
        return carry
    lax.fori_loop(0, nchunk, emit, 0)


def _select(aff_t, cap):
    t = aff_t.shape[1]
    full = pl.BlockSpec((N_EXPERTS, t), lambda i: (0, 0))
    return pl.pallas_call(
        functools.partial(_select_kernel, t=t, cap=cap),
        grid=(1,),
        in_specs=[full],
        out_specs=[full, full],
        out_shape=[jax.ShapeDtypeStruct((N_EXPERTS, t), F32)] * 2,
        scratch_shapes=[pltpu.VMEM((N_EXPERTS, t), I32)],
        compiler_params=_cparams(("arbitrary",)),
        name="select",
    )(aff_t)


def _dispatch_kernel(base_ref, cnt_ref, hn_ref, mask_ref, xs_ref, pos_ref,
                     rank_ref, onehot_ref, buf_ref, xbuf_ref, tail_ref, sem_ref, xsem_ref):
    i = pl.program_id(0)
    last = pl.num_programs(0) - 1
    tb = hn_ref.shape[0]
    slot = i % 2

    @pl.when(i == 0)
    def _():
        tail_ref[...] = jnp.zeros_like(tail_ref)
        xbuf_ref[...] = jnp.zeros_like(xbuf_ref)
        pad = [pltpu.make_async_copy(xbuf_ref, xs_ref.at[e, pl.ds(xs_ref.shape[1] - STRIP, STRIP), :], xsem_ref.at[0])
               for e in range(N_EXPERTS)]
        for cp in pad:
            cp.start()
        for cp in pad:
            cp.wait()

    r_i = lax.broadcasted_iota(I32, (tb, tb), 0)
    c_i = lax.broadcasted_iota(I32, (tb, tb), 1)
    incl = jnp.dot(mask_ref[...].astype(BF16), (r_i <= c_i).astype(BF16), preferred_element_type=F32)
    rank_ref[...] = incl - mask_ref[...]
    strip_row = lax.broadcasted_iota(I32, (STRIP, tb), 0).astype(F32)

    def onehot(e, k):
        off = (base_ref[i * N_EXPERTS + e] % BF16_ROWS).astype(F32)
        target = rank_ref[pl.ds(e, 1), :] + off
        sel = mask_ref[pl.ds(e, 1), :] > 0.0
        return (((strip_row + k * STRIP) == target) & sel).astype(BF16)

    def build(e, carry):
        pos_ref[pl.ds(e, 1), :] = rank_ref[pl.ds(e, 1), :] + base_ref[i * N_EXPERTS + e].astype(F32)
        onehot_ref[pl.ds(pl.multiple_of(e * STRIP, STRIP), STRIP), :] = onehot(e, 0)
        return carry
    for e in range(N_EXPERTS):
        build(e, 0)
    buf_ref[slot] = jnp.dot(onehot_ref[...], hn_ref[...], preferred_element_type=F32).astype(BF16)

    def strip_copy(s_, e, start):
        return pltpu.make_async_copy(buf_ref.at[s_, pl.ds(e * STRIP, STRIP), :],
                                     xs_ref.at[e, pl.ds(start, STRIP), :], sem_ref.at[s_, e])

    @pl.when(i > 0)
    def _():
        for e in range(N_EXPERTS):
            strip_copy(1 - slot, e, 0).wait()

    def finish(e, carry):
        base = base_ref[i * N_EXPERTS + e]
        off = base % BF16_ROWS
        end = off + cnt_ref[i * N_EXPERTS + e]
        row0 = pl.multiple_of(e * STRIP, STRIP)
        head = pl.ds(row0, BF16_ROWS)
        buf_ref[slot, head, :] = buf_ref[slot, head, :] + tail_ref[e]
        group = end // BF16_ROWS * BF16_ROWS
        has_tail = end % BF16_ROWS != 0

        @pl.when(jnp.logical_not(has_tail))
        def _():
            tail_ref[e] = jnp.zeros((BF16_ROWS, D_MODEL), BF16)

        @pl.when(has_tail & (group < STRIP))
        def _():
            tail_ref[e] = buf_ref[slot, pl.ds(pl.multiple_of(row0 + group, BF16_ROWS), BF16_ROWS), :]

        def extra(k, carry):
            xbuf_ref[...] = jnp.dot(onehot(e, k), hn_ref[...], preferred_element_type=F32).astype(BF16)

            @pl.when(has_tail & (group // STRIP == k))
            def _():
                tail_ref[e] = xbuf_ref[pl.ds(pl.multiple_of(group - k * STRIP, BF16_ROWS), BF16_ROWS), :]
            start = pl.multiple_of(base - off + k * STRIP, BF16_ROWS)
            cp = pltpu.make_async_copy(xbuf_ref, xs_ref.at[e, pl.ds(start, STRIP), :], xsem_ref.at[0])
            cp.start()
            cp.wait()
            return carry
        lax.fori_loop(1, jnp.maximum((end + STRIP - 1) // STRIP, 1), extra, 0)
        return carry
    for e in range(N_EXPERTS):
        finish(e, 0)

    for e in range(N_EXPERTS):
        base = base_ref[i * N_EXPERTS + e]
        strip_copy(slot, e, pl.multiple_of(base - base % BF16_ROWS, BF16_ROWS)).start()

    @pl.when(i == last)
    def _():
        for e in range(N_EXPERTS):
            strip_copy(slot, e, 0).wait()


def _dispatch(hn, mask_t, base, cnt, cap):
    t = hn.shape[0]
    tb = TB_ROUTE
    grid_spec = pltpu.PrefetchScalarGridSpec(
        num_scalar_prefetch=2,
        grid=(t // tb,),
        in_specs=[pl.BlockSpec((tb, D_MODEL), lambda i, *_: (i, 0)),
                  pl.BlockSpec((N_EXPERTS, tb), lambda i, *_: (0, i))],
        out_specs=[pl.BlockSpec(memory_space=pl.ANY),
                   pl.BlockSpec((N_EXPERTS, tb), lambda i, *_: (0, i))],
        scratch_shapes=[pltpu.VMEM((N_EXPERTS, tb), F32),
                        pltpu.VMEM((N_EXPERTS * STRIP, tb), BF16),
                        pltpu.VMEM((2, N_EXPERTS * STRIP, D_MODEL), BF16),
                        pltpu.VMEM((STRIP, D_MODEL), BF16),
                        pltpu.VMEM((N_EXPERTS, BF16_ROWS, D_MODEL), BF16),
                        pltpu.SemaphoreType.DMA((2, N_EXPERTS)), pltpu.SemaphoreType.DMA((1,))],
    )
    return pl.pallas_call(
        _dispatch_kernel,
        grid_spec=grid_spec,
        out_shape=[jax.ShapeDtypeStruct((N_EXPERTS, cap + STRIP, D_MODEL), BF16),
                   jax.ShapeDtypeStruct((N_EXPERTS, t), F32)],
        compiler_params=_cparams(("arbitrary",)),
        name="dispatch",
    )(base, cnt, hn, mask_t)


def _ffn_kernel(xs_ref, wg_ref, wu_ref, wd_ref, ye_ref):
    pad_tile = pl.program_id(1) == pl.num_programs(1) - 1

    @pl.when(jnp.logical_not(pad_tile))
    def _():
        x = xs_ref[0]
        g = jnp.dot(x, wg_ref[0], preferred_element_type=F32)
        u = jnp.dot(x, wu_ref[0], preferred_element_type=F32)
        a = (_silu(g) * u).astype(BF16)
        ye_ref[0] = jnp.dot(a, wd_ref[0], preferred_element_type=F32).astype(ye_ref.dtype)

    @pl.when(pad_tile)
    def _():
        ye_ref[...] = jnp.zeros_like(ye_ref)


def _ffn(xs, wg, wu, wd, cap):
    ntile = cap // TM_FFN
    once = pl.Buffered(1)
    return pl.pallas_call(
        _ffn_kernel,
        grid=(N_EXPERTS, ntile + 1),
        in_specs=[pl.BlockSpec((1, TM_FFN, D_MODEL), lambda e, j: (e, jnp.minimum(j, ntile - 1), 0)),
                  pl.BlockSpec((1, D_MODEL, EXPERT_FF), lambda e, j: (e, 0, 0), pipeline_mode=once),
                  pl.BlockSpec((1, D_MODEL, EXPERT_FF), lambda e, j: (e, 0, 0), pipeline_mode=once),
                  pl.BlockSpec((1, EXPERT_FF, D_MODEL), lambda e, j: (e, 0, 0), pipeline_mode=once)],
        out_specs=pl.BlockSpec((1, TM_FFN, D_MODEL), lambda e, j: (e, j, 0)),
        out_shape=jax.ShapeDtypeStruct((N_EXPERTS, cap + TM_FFN, D_MODEL), BF16),
        compiler_params=_cparams(("parallel", "parallel")),
        name="expert_ffn",
    )(xs, wg, wu, wd)


def _combine_kernel(start_ref, nstrip_ref, x1_ref, pos_ref, gate_ref, pe_ref, ye_ref, wple_ref, wpg_ref,
                    gple_ref, gpg_ref, gfin_ref, o_ref, acc_ref, w_ref, buf_ref, sem_ref, xbuf_ref, xsem_ref):
    i = pl.program_id(0)
    ntile = pl.num_programs(0)
    tb = x1_ref.shape[0]
    half = tb // 2
    slot = i % 2
    lane = lax.broadcasted_iota(I32, (half, LANES), 1)
    left = lane < STRIP
    lane_f = lane.astype(F32)

    def strip_copy(s_, tile, e):
        start = pl.multiple_of(start_ref[tile * N_EXPERTS + e], BF16_ROWS)
        return pltpu.make_async_copy(ye_ref.at[e, pl.ds(start, STRIP), :],
                                     buf_ref.at[s_, pl.ds(e * STRIP, STRIP), :], sem_ref.at[s_, e])

    @pl.when(i == 0)
    def _():
        for e in range(N_EXPERTS):
            strip_copy(0, 0, e).start()

    @pl.when(i + 1 < ntile)
    def _():
        for e in range(N_EXPERTS):
            strip_copy(1 - slot, i + 1, e).start()

    def rel_gate(rows, e, width=LANES):
        rel = pos_ref[rows, e:e + 1] - start_ref[i * N_EXPERTS + e].astype(F32)
        return jnp.broadcast_to(rel, (half, width)), jnp.broadcast_to(gate_ref[rows, e:e + 1], (half, width))

    def first_strips(rows):
        for pair in range(N_EXPERTS // 2):
            rel0, g0 = rel_gate(rows, 2 * pair)
            rel1, g1 = rel_gate(rows, 2 * pair + 1)
            target = jnp.where(left, rel0, rel1 + STRIP)
            gate = jnp.where(left, g0, g1)
            w_ref[rows, pair * LANES:(pair + 1) * LANES] = jnp.where(lane_f == target, gate, 0.0).astype(BF16)
        return x1_ref[rows, :] + jnp.dot(w_ref[rows, :], buf_ref[slot], preferred_element_type=F32)

    def finish(rows, x2):
        ple = _rms(jnp.dot(pe_ref[rows, :].astype(BF16), wple_ref[...], preferred_element_type=F32), gple_ref[...])
        pg = _sigmoid(jnp.dot(_rms(x2, gpg_ref[...]).astype(BF16), wpg_ref[...], preferred_element_type=F32))
        o_ref[rows, :] = _rms(x2 + pg * ple, gfin_ref[...])

    assert 2 * STRIP == LANES
    halves = [pl.ds(h * half, half) for h in range(2)]
    more = functools.reduce(jnp.logical_or, [nstrip_ref[i * N_EXPERTS + e] > 1 for e in range(N_EXPERTS)])
    for e in range(N_EXPERTS):
        strip_copy(slot, i, e).wait()

    @pl.when(jnp.logical_not(more))
    def _():
        for rows in halves:
            finish(rows, first_strips(rows))

    @pl.when(more)
    def _():
        for rows in halves:
            acc_ref[rows, :] = first_strips(rows)
        for e in range(N_EXPERTS):
            def extra(k, carry, e=e):
                start = pl.multiple_of(start_ref[i * N_EXPERTS + e] + k * STRIP, BF16_ROWS)
                cp = pltpu.make_async_copy(ye_ref.at[e, pl.ds(start, STRIP), :], xbuf_ref, xsem_ref.at[0])
                cp.start()
                cp.wait()
                strip_lane = (lax.broadcasted_iota(I32, (half, STRIP), 1) + k * STRIP).astype(F32)
                for rows in halves:
                    rel, gate = rel_gate(rows, e, STRIP)
                    w = jnp.where(strip_lane == rel, gate, 0.0).astype(BF16)
                    acc_ref[rows, :] += jnp.dot(w, xbuf_ref[...], preferred_element_type=F32)
                return carry
            lax.fori_loop(1, jnp.maximum(nstrip_ref[i * N_EXPERTS + e], 1), extra, 0)
        for rows in halves:
            finish(rows, acc_ref[rows, :])


def _combine(start, nstrip, x1, pos, gate, pe2, ye, wple, wpg, g_ple, g_pg, g_final):
    t = x1.shape[0]
    tb = TB_ROUTE
    full = lambda a: pl.BlockSpec(a.shape, lambda i, *_: (0,) * a.ndim)
    grid_spec = pltpu.PrefetchScalarGridSpec(
        num_scalar_prefetch=2,
        grid=(t // tb,),
        in_specs=[pl.BlockSpec((tb, D_MODEL), lambda i, *_: (i, 0)),
                  pl.BlockSpec((tb, N_EXPERTS), lambda i, *_: (i, 0)),
                  pl.BlockSpec((tb, N_EXPERTS), lambda i, *_: (i, 0)),
                  pl.BlockSpec((tb, PLE_DIM), lambda i, *_: (i, 0)),
                  pl.BlockSpec(memory_space=pl.ANY),
                  full(wple), full(wpg), full(g_ple), full(g_pg), full(g_final)],
        out_specs=pl.BlockSpec((tb, D_MODEL), lambda i, *_: (i, 0)),
        scratch_shapes=[pltpu.VMEM((tb, D_MODEL), F32),
                        pltpu.VMEM((tb, N_EXPERTS * STRIP), BF16),
                        pltpu.VMEM((2, N_EXPERTS * STRIP, D_MODEL), BF16), pltpu.SemaphoreType.DMA((2, N_EXPERTS)),
                        pltpu.VMEM((STRIP, D_MODEL), BF16), pltpu.SemaphoreType.DMA((1,))],
    )
    return pl.pallas_call(
        _combine_kernel,
        grid_spec=grid_spec,
        out_shape=jax.ShapeDtypeStruct((t, D_MODEL), F32),
        compiler_params=_cparams(("arbitrary",)),
        name="combine",
    )(start, nstrip, x1, pos, gate, pe2, ye, wple, wpg, g_ple, g_pg, g_final)


def _moe(x1, hn, aff_t, pe2, w):
    t = x1.shape[0]
    cap = max(1, CAPACITY_FACTOR * t // N_EXPERTS)
    assert cap % TM_FFN == 0 and t % TB_ROUTE == 0
    mask_t, gate_t = _select(aff_t, cap)
    nt = t // TB_ROUTE
    cnt = jnp.sum(mask_t.reshape(N_EXPERTS, nt, TB_ROUTE), axis=2).astype(I32).T
    base = jnp.cumsum(cnt, axis=0) - cnt
    xs, pos_t = _dispatch(hn, mask_t, base.reshape(-1), cnt.reshape(-1), cap)
    ye = _ffn(xs, w["wg"], w["wu"], w["wd"], cap)
    start = base // BF16_ROWS * BF16_ROWS
    nstrip = jnp.where(cnt > 0, (base - start + cnt + STRIP - 1) // STRIP, 0)
    return _combine(start.reshape(-1), nstrip.reshape(-1), x1, pos_t.T, gate_t.T, pe2, ye,
                    w["wple"], w["wpg"], w["g_ple"], w["g_pg"], w["g_final"])


def _trunk(x, pe, w):
    b, s, _ = x.shape
    t = b * s
    x2 = x.reshape(t, D_MODEL)
    qkv, z, xbc, dt = _in_proj(x2, w["g_mix"], w["wqkv"], w["wz"], w["wxbc"], w["wdt"])
    dt = dt.reshape(SSD_GROUPS, b, s, 2 * HEADS_PER_GROUP)
    attn = _attention(qkv.reshape(b, s, 3 * ATTN_WIDTH), w["bias_tabs"])
    ssd = _ssd(xbc.reshape(b, s, CONV_DIM), z.reshape(b, s, SSD_INNER),
               dt, dt.transpose(0, 1, 3, 2),
               w["conv_w"], w["conv_b"], w["dtb"], w["alog"], w["dsk"], w["gn"])
    x1, hn, aff_t = _out_proj(x2, attn.reshape(t, ATTN_WIDTH), ssd.reshape(t, SSD_INNER),
                              w["wo_a"], w["wo_s"], w["g_ffn"], w["wr_t"])
    y = _moe(x1, hn, aff_t, pe.reshape(t, PLE_DIM), w)
    return y.reshape(b, s, D_MODEL)


def _group_heads(a):
    parts = [jnp.concatenate([a[..., 0, g * HEADS_PER_GROUP:(g + 1) * HEADS_PER_GROUP],
                              a[..., 1, g * HEADS_PER_GROUP:(g + 1) * HEADS_PER_GROUP]], axis=-1)
             for g in range(SSD_GROUPS)]
    return jnp.stack(parts)


def _prepare(rel_bias, g_mix, w_in, conv_w, conv_b, dt_bias, a_log, d_skip, g_ssd, w_out, g_ffn, w_router,
             w_gate, w_up, w_down, g_pg, w_pg, w_ple, g_ple, g_final):
    row = lambda v: v.reshape(1, -1).astype(F32)
    c0 = 3 * ATTN_WIDTH
    c1 = c0 + SSD_INNER
    c2 = c1 + CONV_DIM
    w_dt = w_in[:, c2:].reshape(D_MODEL, 2, SSD_HEADS)
    return dict(
        g_mix=row(g_mix), wqkv=w_in[:, :c0].astype(BF16), wz=w_in[:, c0:c1].astype(BF16),
        wxbc=w_in[:, c1:c2].astype(BF16), wdt=_group_heads(w_dt).astype(BF16),
        bias_tabs=_attn_bias_tables(rel_bias),
        conv_w=conv_w.astype(F32), conv_b=row(conv_b),
        dtb=_group_heads(dt_bias.astype(F32)).reshape(SSD_GROUPS, 1, 2 * HEADS_PER_GROUP),
        alog=_group_heads(a_log.astype(F32)).reshape(SSD_GROUPS, 1, 2 * HEADS_PER_GROUP),
        dsk=row(jnp.repeat(d_skip, SSD_HEAD_DIM)), gn=row(g_ssd),
        wo_a=w_out[:ATTN_WIDTH].astype(BF16), wo_s=w_out[ATTN_WIDTH:].astype(BF16),
        g_ffn=row(g_ffn), wr_t=w_router.T.astype(F32),
        wg=w_gate.astype(BF16), wu=w_up.astype(BF16), wd=w_down.astype(BF16),
        g_pg=row(g_pg), wpg=w_pg.astype(BF16), wple=w_ple.astype(BF16), g_ple=row(g_ple), g_final=row(g_final),
    )


def kernel(x_prompt, x_sample, p_prompt, p_sample, rel_bias, g_mix, w_in, conv_w, conv_b, dt_bias, a_log, d_skip,
           g_ssd, w_out, g_ffn, w_router, w_gate, w_up, w_down, g_pg, w_pg, w_ple, g_ple, g_final):
    assert g_mix.shape[0] == 1, "single-layer trunk"
    w = _prepare(rel_bias, g_mix[0], w_in[0], conv_w[0], conv_b[0], dt_bias[0], a_log[0], d_skip[0], g_ssd[0],
                 w_out[0], g_ffn[0], w_router[0], w_gate[0], w_up[0], w_down[0], g_pg[0], w_pg[0], w_ple[0],
                 g_ple[0], g_final)
    return _trunk(x_prompt, p_prompt[0], w), _trunk(x_sample, p_sample[0], w)
```

```python
import functools
import math

import jax
import jax.numpy as jnp
import numpy as np
from jax import lax
from jax.experimental import pallas as pl
from jax.experimental.pallas import tpu as pltpu

F32 = jnp.float32
BF16 = jnp.bfloat16
I32 = jnp.int32

D_MODEL = 1024
PLE_DIM = 256
N_ATTN_HEADS = 8
ATTN_HEAD_DIM = 64
ATTN_WIDTH = N_ATTN_HEADS * ATTN_HEAD_DIM
DILATED_PATTERNS = ((128, 1), (512, 4), (2048, 16))
REL_BUCKETS = 32
REL_MAX_DISTANCE = 1024
SSD_HEADS = 8
SSD_HEAD_DIM = 64
SSD_INNER = SSD_HEADS * SSD_HEAD_DIM
SSD_GROUPS = 2
SSD_STATE = 128
SSD_CONV = 5
SSD_CHUNK = 128
CONV_DIM = SSD_INNER + 2 * SSD_GROUPS * SSD_STATE
N_EXPERTS = 16
EXPERT_FF = 2816
CAPACITY_FACTOR = 2
EPS = 1e-6

LANES = 128
SUBLANES = 8
BF16_ROWS = 16
VMEM_LIMIT = 56 * 1024 * 1024
NEG = -1e30

HEADS_PER_GROUP = SSD_HEADS // SSD_GROUPS
GROUP_COLS = SSD_INNER // SSD_GROUPS
ATT_HALF = 64
ATT_BQ = 128
ATT_KW = 256
TM_PROJ = 512
SSD_UNROLL = 4
TB_ROUTE = 256
STRIP = 64
TM_FFN = 512
SEL_CHUNK = 1024


def _cparams(sem):
    return pltpu.CompilerParams(dimension_semantics=sem, vmem_limit_bytes=VMEM_LIMIT)


def _rms(x, g):
    return x * lax.rsqrt(jnp.mean(x * x, axis=-1, keepdims=True) + EPS) * g


def _sigmoid(x):
    return 0.5 * jnp.tanh(0.5 * x) + 0.5


def _silu(x):
    return x * _sigmoid(x)


def _in_proj_kernel(x_ref, g_ref, wqkv_ref, wz_ref, wxbc_ref, wdt_ref, qkv_ref, z_ref, xbc_ref, dt_ref, dtt_ref):
    h = _rms(x_ref[...], g_ref[...]).astype(BF16)
    qkv_ref[...] = jnp.dot(h, wqkv_ref[...], preferred_element_type=F32)
    z_ref[...] = jnp.dot(h, wz_ref[...], preferred_element_type=F32)
    xbc_ref[...] = jnp.dot(h, wxbc_ref[...], preferred_element_type=F32)
    for g in range(SSD_GROUPS):
        dt = jnp.dot(h, wdt_ref[g], preferred_element_type=F32)
        dt_ref[g] = dt
        dtt_ref[g] = dt.T


def _in_proj(x2, g_mix, wqkv, wz, wxbc, wdt):
    t = x2.shape[0]
    tm = TM_PROJ
    full = lambda a: pl.BlockSpec(a.shape, lambda i: (0,) * a.ndim)
    return pl.pallas_call(
        _in_proj_kernel,
        grid=(t // tm,),
        in_specs=[pl.BlockSpec((tm, D_MODEL), lambda i: (i, 0)), full(g_mix), full(wqkv), full(wz), full(wxbc), full(wdt)],
        out_specs=[
            pl.BlockSpec((tm, 3 * ATTN_WIDTH), lambda i: (i, 0)),
            pl.BlockSpec((tm, SSD_INNER), lambda i: (i, 0)),
            pl.BlockSpec((tm, CONV_DIM), lambda i: (i, 0)),
            pl.BlockSpec((SSD_GROUPS, tm, 2 * HEADS_PER_GROUP), lambda i: (0, i, 0)),
            pl.BlockSpec((SSD_GROUPS, 2 * HEADS_PER_GROUP, tm), lambda i: (0, 0, i)),
        ],
        out_shape=[
            jax.ShapeDtypeStruct((t, 3 * ATTN_WIDTH), F32),
            jax.ShapeDtypeStruct((t, SSD_INNER), F32),
            jax.ShapeDtypeStruct((t, CONV_DIM), F32),
            jax.ShapeDtypeStruct((SSD_GROUPS, t, 2 * HEADS_PER_GROUP), F32),
            jax.ShapeDtypeStruct((SSD_GROUPS, 2 * HEADS_PER_GROUP, t), F32),
        ],
        compiler_params=_cparams(("parallel",)),
        name="in_proj",
    )(x2, g_mix, wqkv, wz, wxbc, wdt)


def _t5_bucket(rel):
    half = REL_BUCKETS // 2
    max_exact = half // 2
    n = np.abs(rel)
    large = max_exact + (np.log(np.maximum(n, 1) / max_exact) / math.log(REL_MAX_DISTANCE / max_exact) * (half - max_exact)).astype(np.int32)
    large = np.minimum(large, half - 1)
    return (np.where(rel > 0, half, 0) + np.where(n < max_exact, n, large)).astype(np.int32)


_ATT_WINDOW_OFFSETS = (0, -ATT_HALF, -(ATT_KW - ATT_BQ))


def _attn_bias_tables(rel_bias):
    buckets = []
    for window, dilation in DILATED_PATTERNS:
        assert window // (2 * dilation) == ATT_HALF
        for off in _ATT_WINDOW_OFFSETS:
            m = np.arange(ATT_KW)[None, :] + off - np.arange(ATT_BQ)[:, None]
            buckets.append(np.where(np.abs(m) <= ATT_HALF, _t5_bucket(np.clip(m, -ATT_HALF, ATT_HALF) * dilation), -1))
    bucket = jnp.asarray(np.stack(buckets).reshape(len(DILATED_PATTERNS), len(_ATT_WINDOW_OFFSETS), 1, ATT_BQ, ATT_KW))
    tab = jnp.full(bucket.shape[:2] + (N_ATTN_HEADS, ATT_BQ, ATT_KW), NEG, F32)
    for bkt in range(REL_BUCKETS):
        tab = jnp.where(bucket == bkt, rel_bias[bkt].astype(F32)[None, None, :, None, None], tab)
    return tab


def _attn_kernel(q_ref, k_ref, v_ref, bias_ref, o_ref, acc_ref, m_ref, l_ref, *, s):
    lane = lax.broadcasted_iota(I32, (ATT_BQ, LANES), 1)
    first_head = lane < ATTN_HEAD_DIM
    scale = ATTN_HEAD_DIM ** -0.5
    nblk = s // ATT_BQ

    def rows(start, size, d):
        return pl.ds(start, size) if d == 1 else pl.ds(start, size, stride=d)

    def block(p, d, f):
        n = s // d
        nq = n // ATT_BQ
        kw = min(ATT_KW, n)
        r = f // nq
        q0 = (f % nq) * ATT_BQ
        ks = jnp.clip(q0 - ATT_HALF, 0, n - kw)
        kind = (q0 - ks) // ATT_HALF
        q_rows = rows(r + d * q0, ATT_BQ, d)
        k_rows = rows(r + d * ks, kw, d)
        qb = q_ref[0, q_rows, :] * scale
        kb = k_ref[0, k_rows, :].astype(BF16)
        vb = v_ref[0, k_rows, :].astype(BF16)
        outs, ms, ls = [], [], []
        for hh in range(2):
            sel = first_head if hh == 0 else jnp.logical_not(first_head)
            qm = jnp.where(sel, qb, 0.0).astype(BF16)
            sc = lax.dot_general(qm, kb, (((1,), (1,)), ((), ())), preferred_element_type=F32)
            sc = sc + bias_ref[p, kind, hh, :, :kw]
            mx = jnp.max(sc, axis=-1, keepdims=True)
            pe = jnp.exp(sc - mx)
            ls.append(jnp.sum(pe, axis=-1, keepdims=True))
            ms.append(mx)
            outs.append(jnp.dot(pe.astype(BF16), vb, preferred_element_type=F32))
        acc_ref[p, q_rows, :] = jnp.where(first_head, outs[0], outs[1])
        m_ref[p, q_rows, :] = jnp.where(first_head, ms[0], ms[1])
        l_ref[p, q_rows, :] = jnp.where(first_head, ls[0], ls[1])

    for p, (_, d) in enumerate(DILATED_PATTERNS):
        for f in range(nblk):
            block(p, d, f)

    def finish(i, carry):
        sl = pl.ds(pl.multiple_of(i * ATT_BQ, ATT_BQ), ATT_BQ)
        ms = [m_ref[p, sl, :] for p in range(len(DILATED_PATTERNS))]
        mx = functools.reduce(jnp.maximum, ms)
        ws = [jnp.exp(m - mx) for m in ms]
        num = sum(w * acc_ref[p, sl, :] for p, w in enumerate(ws))
        den = sum(w * l_ref[p, sl, :] for p, w in enumerate(ws))
        o_ref[0, sl, :] = (num / den).astype(o_ref.dtype)
        return carry
    lax.fori_loop(0, nblk, finish, 0)


def _attention(qkv3, bias_tabs):
    b, s, _ = qkv3.shape
    assert s % (16 * ATT_BQ) == 0
    npairs = N_ATTN_HEADS // 2
    col = lambda base: pl.BlockSpec((1, s, LANES), lambda bi, hp: (bi, 0, base + hp))
    return pl.pallas_call(
        functools.partial(_attn_kernel, s=s),
        grid=(b, npairs),
        in_specs=[col(0), col(npairs), col(2 * npairs),
                  pl.BlockSpec((len(DILATED_PATTERNS), len(_ATT_WINDOW_OFFSETS), 2, ATT_BQ, ATT_KW), lambda bi, hp: (0, 0, hp, 0, 0))],
        out_specs=pl.BlockSpec((1, s, LANES), lambda bi, hp: (bi, 0, hp)),
        out_shape=jax.ShapeDtypeStruct((b, s, ATTN_WIDTH), BF16),
        scratch_shapes=[pltpu.VMEM((len(DILATED_PATTERNS), s, LANES), F32)] * 3,
        compiler_params=_cparams(("parallel", "parallel")),
        name="attention",
    )(qkv3, qkv3, qkv3, bias_tabs)


def _expand_heads(v):
    rows = v.shape[0]
    head = lax.broadcasted_iota(I32, (rows, GROUP_COLS), 1) // SSD_HEAD_DIM
    out = jnp.broadcast_to(v[:, 0:1], (rows, GROUP_COLS))
    for j in range(1, HEADS_PER_GROUP):
        out = jnp.where(head == j, v[:, j:j + 1], out)
    return out


def _ssd_kernel(xs_ref, b_ref, c_ref, z_ref, dt_ref, dtt_ref, wx_ref, wb_ref, wc_ref, bx_ref, bb_ref, bc_ref,
                dtb_ref, alog_ref, dtbc_ref, alogc_ref, dsk_ref, gn_ref, o_ref, xc_ref, bcv_ref, ccv_ref, y_ref, h_ref, *, s):
    L = SSD_CHUNK
    nc = s // L
    row = lax.broadcasted_iota(I32, (L, L), 0)
    colm = lax.broadcasted_iota(I32, (L, L), 1)
    lower = row >= colm
    upper = row <= colm
    tril = lower.astype(F32)
    triu = upper.astype(F32)
    head_of_lane = lax.broadcasted_iota(I32, (L, GROUP_COLS), 1) // SSD_HEAD_DIM
    a_neg = -jnp.exp(alog_ref[0])
    dt_bias = dtb_ref[0]
    a_neg_col = -jnp.exp(alogc_ref[0])
    dt_bias_col = dtbc_ref[0]

    def conv(ref, w_ref, bias_ref, c, t0):
        main = ref[0, pl.ds(t0, L), :]
        p0 = pl.multiple_of(jnp.maximum(t0 - SUBLANES, 0), SUBLANES)
        n0 = pl.multiple_of(jnp.minimum(t0 + L, s - SUBLANES), SUBLANES)
        prev = ref[0, pl.ds(p0, SUBLANES), :] * jnp.where(c > 0, 1.0, 0.0)
        nxt = ref[0, pl.ds(n0, SUBLANES), :] * jnp.where(c < nc - 1, 1.0, 0.0)
        ext = jnp.concatenate([prev, main, nxt], axis=0)
        acc = jnp.broadcast_to(bias_ref[...], main.shape)
        for k in range(SSD_CONV):
            lo = SUBLANES - SSD_CONV // 2 + k
            acc = acc + w_ref[k:k + 1, :] * ext[lo:lo + L, :]
        return _silu(acc)

    def chunk(c, direction):
        t0 = pl.multiple_of(c * L, L)
        if direction == 0:
            xs = conv(xs_ref, wx_ref, bx_ref, c, t0)
            bm = conv(b_ref, wb_ref, bb_ref, c, t0)
            cm = conv(c_ref, wc_ref, bc_ref, c, t0)
            xc_ref[pl.ds(t0, L), :] = xs
            bcv_ref[pl.ds(t0, L), :] = bm
            ccv_ref[pl.ds(t0, L), :] = cm
        else:
            xs = xc_ref[pl.ds(t0, L), :]
            bm = bcv_ref[pl.ds(t0, L), :]
            cm = ccv_ref[pl.ds(t0, L), :]
        x = dt_ref[0, 0, pl.ds(t0, L), :] + dt_bias
        dt_all = jnp.maximum(x, 0.0) + jnp.log1p(jnp.exp(-jnp.abs(x)))
        a_all = dt_all * a_neg
        lo = direction * HEADS_PER_GROUP
        dt = dt_all[:, lo:lo + HEADS_PER_GROUP]
        a = a_all[:, lo:lo + HEADS_PER_GROUP]
        cum = jnp.dot(tril if direction == 0 else triu, a, preferred_element_type=F32, precision=lax.Precision.HIGHEST)
        total = cum[L - 1:L, :] if direction == 0 else cum[0:1, :]
        xr = dtt_ref[0, :, pl.ds(t0, L)] + dt_bias_col
        a_row = (jnp.maximum(xr, 0.0) + jnp.log1p(jnp.exp(-jnp.abs(xr)))) * a_neg_col
        cum_row = jnp.dot(a_row, triu if direction == 0 else tril, preferred_element_type=F32,
                          precision=lax.Precision.HIGHEST)
        mask = lower if direction == 0 else upper
        bmb = bm.astype(BF16)
        cmb = cm.astype(BF16)
        cb = lax.dot_general(cmb, bmb, (((1,), (1,)), ((), ())), preferred_element_type=F32)
        xdt = (xs * _expand_heads(dt)).astype(BF16)
        y = jnp.zeros((L, GROUP_COLS), F32)
        for j in range(HEADS_PER_GROUP):
            colb = jnp.broadcast_to(cum[:, j:j + 1], (L, L))
            rowb = jnp.broadcast_to(cum_row[lo + j:lo + j + 1, :], (L, L))
            seg = colb - rowb
            dec = jnp.exp(jnp.where(mask, seg, NEG))
            yj = jnp.dot((cb * dec).astype(BF16), xdt, preferred_element_type=F32)
            y = jnp.where(head_of_lane == j, yj, y)
        hprev = h_ref[...]
        y = y + jnp.dot(cmb, hprev.astype(BF16), preferred_element_type=F32) * _expand_heads(jnp.exp(cum))
        xw = (xs * _expand_heads(dt * jnp.exp(total - cum))).astype(BF16)
        st = lax.dot_general(bmb, xw, (((0,), (0,)), ((), ())), preferred_element_type=F32)
        h_ref[...] = hprev * _expand_heads(jnp.exp(total)) + st
        if direction == 0:
            y_ref[pl.ds(t0, L), :] = y
        else:
            y = y_ref[pl.ds(t0, L), :] + y + dsk_ref[...] * xs
            y = y * _silu(z_ref[0, pl.ds(t0, L), :])
            o_ref[0, pl.ds(t0, L), :] = _rms(y, gn_ref[...]).astype(o_ref.dtype)

    h_ref[...] = jnp.zeros_like(h_ref)

    def fwd(c, carry):
        for u in range(SSD_UNROLL):
            chunk(c * SSD_UNROLL + u, 0)
        return carry
    lax.fori_loop(0, nc // SSD_UNROLL, fwd, 0)
    h_ref[...] = jnp.zeros_like(h_ref)

    def bwd(i, carry):
        for u in range(SSD_UNROLL):
            chunk(nc - 1 - (i * SSD_UNROLL + u), 1)
        return carry
    lax.fori_loop(0, nc // SSD_UNROLL, bwd, 0)


def _ssd(xbc3, z3, dt4, dtt4, conv_w, conv_b, dtb, alog, dsk, gn):
    b, s, _ = xbc3.shape
    g_off_b = SSD_INNER // SSD_STATE
    g_off_c = g_off_b + SSD_GROUPS
    xcol = pl.BlockSpec((1, s, GROUP_COLS), lambda bi, g: (bi, 0, g))
    bcol = pl.BlockSpec((1, s, SSD_STATE), lambda bi, g: (bi, 0, g_off_b + g))
    ccol = pl.BlockSpec((1, s, SSD_STATE), lambda bi, g: (bi, 0, g_off_c + g))
    return pl.pallas_call(
        functools.partial(_ssd_kernel, s=s),
        grid=(b, SSD_GROUPS),
        in_specs=[
            xcol, bcol, ccol,
            pl.BlockSpec((1, s, GROUP_COLS), lambda bi, g: (bi, 0, g)),
            pl.BlockSpec((1, 1, s, 2 * HEADS_PER_GROUP), lambda bi, g: (g, bi, 0, 0)),
            pl.BlockSpec((1, 2 * HEADS_PER_GROUP, s), lambda bi, g: (g, 0, bi)),
            pl.BlockSpec((SSD_CONV, GROUP_COLS), lambda bi, g: (0, g)),
            pl.BlockSpec((SSD_CONV, SSD_STATE), lambda bi, g: (0, g_off_b + g)),
            pl.BlockSpec((SSD_CONV, SSD_STATE), lambda bi, g: (0, g_off_c + g)),
            pl.BlockSpec((1, GROUP_COLS), lambda bi, g: (0, g)),
            pl.BlockSpec((1, SSD_STATE), lambda bi, g: (0, g_off_b + g)),
            pl.BlockSpec((1, SSD_STATE), lambda bi, g: (0, g_off_c + g)),
            pl.BlockSpec((1, 1, 2 * HEADS_PER_GROUP), lambda bi, g: (g, 0, 0)),
            pl.BlockSpec((1, 1, 2 * HEADS_PER_GROUP), lambda bi, g: (g, 0, 0)),
            pl.BlockSpec((1, 2 * HEADS_PER_GROUP, 1), lambda bi, g: (g, 0, 0)),
            pl.BlockSpec((1, 2 * HEADS_PER_GROUP, 1), lambda bi, g: (g, 0, 0)),
            pl.BlockSpec((1, GROUP_COLS), lambda bi, g: (0, g)),
            pl.BlockSpec((1, GROUP_COLS), lambda bi, g: (0, g)),
        ],
        out_specs=pl.BlockSpec((1, s, GROUP_COLS), lambda bi, g: (bi, 0, g)),
        out_shape=jax.ShapeDtypeStruct((b, s, SSD_INNER), BF16),
        scratch_shapes=[
            pltpu.VMEM((s, GROUP_COLS), F32), pltpu.VMEM((s, SSD_STATE), F32), pltpu.VMEM((s, SSD_STATE), F32),
            pltpu.VMEM((s, GROUP_COLS), F32), pltpu.VMEM((SSD_STATE, GROUP_COLS), F32),
        ],
        compiler_params=_cparams(("parallel", "parallel")),
        name="ssd",
    )(xbc3, xbc3, xbc3, z3, dt4, dtt4, conv_w, conv_w, conv_w, conv_b, conv_b, conv_b, dtb, alog,
      dtb.reshape(SSD_GROUPS, -1, 1), alog.reshape(SSD_GROUPS, -1, 1), dsk, gn)


def _out_proj_kernel(x_ref, attn_ref, ssd_ref, wa_ref, ws_ref, g_ref, wr_ref, x1_ref, hn_ref, aff_ref):
    x1 = x_ref[...] + jnp.dot(attn_ref[...], wa_ref[...], preferred_element_type=F32) \
        + jnp.dot(ssd_ref[...], ws_ref[...], preferred_element_type=F32)
    x1_ref[...] = x1
    hn = _rms(x1, g_ref[...])
    hn_ref[...] = hn.astype(BF16)
    logits = lax.dot_general(wr_ref[...], hn, (((1,), (1,)), ((), ())), preferred_element_type=F32,
                             precision=lax.Precision.HIGHEST)
    e = jnp.exp(logits - jnp.max(logits, axis=0, keepdims=True))
    aff_ref[...] = e / jnp.sum(e, axis=0, keepdims=True)


def _out_proj(x2, attn2, ssd2, wa, ws, g_ffn, wr_t):
    t = x2.shape[0]
    tm = TM_PROJ
    full = lambda a: pl.BlockSpec(a.shape, lambda i: (0,) * a.ndim)
    return pl.pallas_call(
        _out_proj_kernel,
        grid=(t // tm,),
        in_specs=[pl.BlockSpec((tm, D_MODEL), lambda i: (i, 0)),
                  pl.BlockSpec((tm, ATTN_WIDTH), lambda i: (i, 0)),
                  pl.BlockSpec((tm, SSD_INNER), lambda i: (i, 0)),
                  full(wa), full(ws), full(g_ffn), full(wr_t)],
        out_specs=[pl.BlockSpec((tm, D_MODEL), lambda i: (i, 0)),
                   pl.BlockSpec((tm, D_MODEL), lambda i: (i, 0)),
                   pl.BlockSpec((N_EXPERTS, tm), lambda i: (0, i))],
        out_shape=[jax.ShapeDtypeStruct((t, D_MODEL), F32),
                   jax.ShapeDtypeStruct((t, D_MODEL), BF16),
                   jax.ShapeDtypeStruct((N_EXPERTS, t), F32)],
        compiler_params=_cparams(("parallel",)),
        name="out_proj",
    )(x2, attn2, ssd2, wa, ws, g_ffn, wr_t)


def _select_kernel(aff_ref, mask_ref, gate_ref, bits_ref, *, t, cap):
    nchunk = t // SEL_CHUNK
    lane = lax.broadcasted_iota(I32, (N_EXPERTS, SEL_CHUNK), 1)

    def to_bits(i, carry):
        sl = pl.ds(pl.multiple_of(i * SEL_CHUNK, SEL_CHUNK), SEL_CHUNK)
        bits_ref[:, sl] = lax.bitcast_convert_type(aff_ref[:, sl], I32)
        return carry
    lax.fori_loop(0, nchunk, to_bits, 0)

    def count(pred):
        def body(i, acc):
            sl = pl.ds(pl.multiple_of(i * SEL_CHUNK, SEL_CHUNK), SEL_CHUNK)
            return acc + pred(bits_ref[:, sl], lane + i * SEL_CHUNK).astype(F32)
        acc = lax.fori_loop(0, nchunk, body, jnp.zeros((N_EXPERTS, SEL_CHUNK), F32))
        return jnp.sum(acc, axis=1, keepdims=True)

    thr = jnp.zeros((N_EXPERTS, 1), I32)
    for bit in range(30, -1, -1):
        cand = thr | (1 << bit)
        cnt = count(lambda b, idx, cand=cand: b >= cand)
        thr = jnp.where(cnt >= cap, cand, thr)
    need = cap - count(lambda b, idx: b > thr)
    bound = jnp.zeros((N_EXPERTS, 1), I32)
    for bit in range(t.bit_length() - 1, -1, -1):
        cand = bound | (1 << bit)
        cnt = count(lambda b, idx, cand=cand: (b == thr) & (idx < cand))
        bound = jnp.where(cnt <= need, cand, bound)

    def emit(i, carry):
        sl = pl.ds(pl.multiple_of(i * SEL_CHUNK, SEL_CHUNK), SEL_CHUNK)
        b = bits_ref[:, sl]
        sel = (b > thr) | ((b == thr) & ((lane + i * SEL_CHUNK) < bound))
        mask_ref[:, sl] = sel.astype(F32)
        gate_ref[:, sl] = jnp.where(sel, aff_ref[:, sl], 0.0)
        return carry
    lax.fori_loop(0, nchunk, emit, 0)


def _select(aff_t, cap):
    t = aff_t.shape[1]
    full = pl.BlockSpec((N_EXPERTS, t), lambda i: (0, 0))
    return pl.pallas_call(
        functools.partial(_select_kernel, t=t, cap=cap),
        grid=(1,),
        in_specs=[full],
        out_specs=[full, full],
        out_shape=[jax.ShapeDtypeStruct((N_EXPERTS, t), F32)] * 2,
        scratch_shapes=[pltpu.VMEM((N_EXPERTS, t), I32)],
        compiler_params=_cparams(("arbitrary",)),
        name="select",
    )(aff_t)


def _dispatch_kernel(base_ref, cnt_ref, hn_ref, mask_ref, xs_ref, pos_ref,
                     rank_ref, onehot_ref, buf_ref, xbuf_ref, tail_ref, sem_ref, xsem_ref):
    i = pl.program_id(0)
    last = pl.num_programs(0) - 1
    tb = hn_ref.shape[0]
    slot = i % 2

    @pl.when(i == 0)
    def _():
        tail_ref[...] = jnp.zeros_like(tail_ref)
        xbuf_ref[...] = jnp.zeros_like(xbuf_ref)
        pad = [pltpu.make_async_copy(xbuf_ref, xs_ref.at[e, pl.ds(xs_ref.shape[1] - STRIP, STRIP), :], xsem_ref.at[0])
               for e in range(N_EXPERTS)]
        for cp in pad:
            cp.start()
        for cp in pad:
            cp.wait()

    r_i = lax.broadcasted_iota(I32, (tb, tb), 0)
    c_i = lax.broadcasted_iota(I32, (tb, tb), 1)
    incl = jnp.dot(mask_ref[...].astype(BF16), (r_i <= c_i).astype(BF16), preferred_element_type=F32)
    rank_ref[...] = incl - mask_ref[...]
    strip_row = lax.broadcasted_iota(I32, (STRIP, tb), 0).astype(F32)

    def onehot(e, k):
        off = (base_ref[i * N_EXPERTS + e] % BF16_ROWS).astype(F32)
        target = rank_ref[pl.ds(e, 1), :] + off
        sel = mask_ref[pl.ds(e, 1), :] > 0.0
        return (((strip_row + k * STRIP) == target) & sel).astype(BF16)

    def build(e, carry):
        pos_ref[pl.ds(e, 1), :] = rank_ref[pl.ds(e, 1), :] + base_ref[i * N_EXPERTS + e].astype(F32)
        onehot_ref[pl.ds(pl.multiple_of(e * STRIP, STRIP), STRIP), :] = onehot(e, 0)
        return carry
    for e in range(N_EXPERTS):
        build(e, 0)
    buf_ref[slot] = jnp.dot(onehot_ref[...], hn_ref[...], preferred_element_type=F32).astype(BF16)

    def strip_copy(s_, e, start):
        return pltpu.make_async_copy(buf_ref.at[s_, pl.ds(e * STRIP, STRIP), :],
                                     xs_ref.at[e, pl.ds(start, STRIP), :], sem_ref.at[s_, e])

    @pl.when(i > 0)
    def _():
        for e in range(N_EXPERTS):
            strip_copy(1 - slot, e, 0).wait()

    def finish(e, carry):
        base = base_ref[i * N_EXPERTS + e]
        off = base % BF16_ROWS
        end = off + cnt_ref[i * N_EXPERTS + e]
        row0 = pl.multiple_of(e * STRIP, STRIP)
        head = pl.ds(row0, BF16_ROWS)
        buf_ref[slot, head, :] = buf_ref[slot, head, :] + tail_ref[e]
        group = end // BF16_ROWS * BF16_ROWS
        has_tail = end % BF16_ROWS != 0

        @pl.when(jnp.logical_not(has_tail))
        def _():
            tail_ref[e] = jnp.zeros((BF16_ROWS, D_MODEL), BF16)

        @pl.when(has_tail & (group < STRIP))
        def _():
            tail_ref[e] = buf_ref[slot, pl.ds(pl.multiple_of(row0 + group, BF16_ROWS), BF16_ROWS), :]

        def extra(k, carry):
            xbuf_ref[...] = jnp.dot(onehot(e, k), hn_ref[...], preferred_element_type=F32).astype(BF16)

            @pl.when(has_tail & (group // STRIP == k))
            def _():
                tail_ref[e] = xbuf_ref[pl.ds(pl.multiple_of(group - k * STRIP, BF16_ROWS), BF16_ROWS), :]
            start = pl.multiple_of(base - off + k * STRIP, BF16_ROWS)
            cp = pltpu.make_async_copy(xbuf_ref, xs_ref.at[e, pl.ds(start, STRIP), :], xsem_ref.at[0])
            cp.start()
            cp.wait()
            return carry
        lax.fori_loop(1, jnp.maximum((end + STRIP - 1) // STRIP, 1), extra, 0)
        return carry
    for e in range(N_EXPERTS):
        finish(e, 0)

    for e in range(N_EXPERTS):
        base = base_ref[i * N_EXPERTS + e]
        strip_copy(slot, e, pl.multiple_of(base - base % BF16_ROWS, BF16_ROWS)).start()

    @pl.when(i == last)
    def _():
        for e in range(N_EXPERTS):
            strip_copy(slot, e, 0).wait()


def _dispatch(hn, mask_t, base, cnt, cap):
    t = hn.shape[0]
    tb = TB_ROUTE
    grid_spec = pltpu.PrefetchScalarGridSpec(
        num_scalar_prefetch=2,
        grid=(t // tb,),
        in_specs=[pl.BlockSpec((tb, D_MODEL), lambda i, *_: (i, 0)),
                  pl.BlockSpec((N_EXPERTS, tb), lambda i, *_: (0, i))],
        out_specs=[pl.BlockSpec(memory_space=pl.ANY),
                   pl.BlockSpec((N_EXPERTS, tb), lambda i, *_: (0, i))],
        scratch_shapes=[pltpu.VMEM((N_EXPERTS, tb), F32),
                        pltpu.VMEM((N_EXPERTS * STRIP, tb), BF16),
                        pltpu.VMEM((2, N_EXPERTS * STRIP, D_MODEL), BF16),
                        pltpu.VMEM((STRIP, D_MODEL), BF16),
                        pltpu.VMEM((N_EXPERTS, BF16_ROWS, D_MODEL), BF16),
                        pltpu.SemaphoreType.DMA((2, N_EXPERTS)), pltpu.SemaphoreType.DMA((1,))],
    )
    return pl.pallas_call(
        _dispatch_kernel,
        grid_spec=grid_spec,
        out_shape=[jax.ShapeDtypeStruct((N_EXPERTS, cap + STRIP, D_MODEL), BF16),
                   jax.ShapeDtypeStruct((N_EXPERTS, t), F32)],
        compiler_params=_cparams(("arbitrary",)),
        name="dispatch",
    )(base, cnt, hn, mask_t)


def _ffn_kernel(xs_ref, wg_ref, wu_ref, wd_ref, ye_ref):
    pad_tile = pl.program_id(1) == pl.num_programs(1) - 1

    @pl.when(jnp.logical_not(pad_tile))
    def _():
        x = xs_ref[0]
        g = jnp.dot(x, wg_ref[0], preferred_element_type=F32)
        u = jnp.dot(x, wu_ref[0], preferred_element_type=F32)
        a = (_silu(g) * u).astype(BF16)
        ye_ref[0] = jnp.dot(a, wd_ref[0], preferred_element_type=F32).astype(ye_ref.dtype)

    @pl.when(pad_tile)
    def _():
        ye_ref[...] = jnp.zeros_like(ye_ref)


def _ffn(xs, wg, wu, wd, cap):
    ntile = cap // TM_FFN
    once = pl.Buffered(1)
    return pl.pallas_call(
        _ffn_kernel,
        grid=(N_EXPERTS, ntile + 1),
        in_specs=[pl.BlockSpec((1, TM_FFN, D_MODEL), lambda e, j: (e, jnp.minimum(j, ntile - 1), 0)),
                  pl.BlockSpec((1, D_MODEL, EXPERT_FF), lambda e, j: (e, 0, 0), pipeline_mode=once),
                  pl.BlockSpec((1, D_MODEL, EXPERT_FF), lambda e, j: (e, 0, 0), pipeline_mode=once),
                  pl.BlockSpec((1, EXPERT_FF, D_MODEL), lambda e, j: (e, 0, 0), pipeline_mode=once)],
        out_specs=pl.BlockSpec((1, TM_FFN, D_MODEL), lambda e, j: (e, j, 0)),
        out_shape=jax.ShapeDtypeStruct((N_EXPERTS, cap + TM_FFN, D_MODEL), BF16),
        compiler_params=_cparams(("parallel", "parallel")),
        name="expert_ffn",
    )(xs, wg, wu, wd)


def _combine_kernel(start_ref, nstrip_ref, x1_ref, pos_ref, gate_ref, pe_ref, ye_ref, wple_ref, wpg_ref,
                    gple_ref, gpg_ref, gfin_ref, o_ref, acc_ref, w_ref, buf_ref, sem_ref, xbuf_ref, xsem_ref):
    i = pl.program_id(0)
    ntile = pl.num_programs(0) - 1
    tile = jnp.minimum(i, ntile - 1)
    tb = x1_ref.shape[0]
    slot = tile % 2
    lane = lax.broadcasted_iota(I32, (tb, LANES), 1)
    left = lane < STRIP
    lane_f = lane.astype(F32)

    def strip_copy(s_, t_, e):
        start = pl.multiple_of(start_ref[t_ * N_EXPERTS + e], BF16_ROWS)
        return pltpu.make_async_copy(ye_ref.at[e, pl.ds(start, STRIP), :],
                                     buf_ref.at[s_, pl.ds(e * STRIP, STRIP), :], sem_ref.at[s_, e])

    @pl.when(i == 0)
    def _():
        acc_ref[1] = jnp.zeros((tb, D_MODEL), F32)
        for e in range(N_EXPERTS):
            strip_copy(0, 0, e).start()

    @pl.when(i + 1 < ntile)
    def _():
        for e in range(N_EXPERTS):
            strip_copy(1 - slot, i + 1, e).start()

    @pl.when(i < ntile)
    def _():
        for e in range(N_EXPERTS):
            strip_copy(slot, i, e).wait()

    def rel_gate(e, width=LANES):
        rel = pos_ref[:, e:e + 1] - start_ref[tile * N_EXPERTS + e].astype(F32)
        return jnp.broadcast_to(rel, (tb, width)), jnp.broadcast_to(gate_ref[:, e:e + 1], (tb, width))

    x2 = x1_ref[...] + acc_ref[(i + 1) % 2]
    ple = _rms(jnp.dot(pe_ref[...].astype(BF16), wple_ref[...], preferred_element_type=F32), gple_ref[...])
    pg = _sigmoid(jnp.dot(_rms(x2, gpg_ref[...]).astype(BF16), wpg_ref[...], preferred_element_type=F32))
    o_ref[...] = _rms(x2 + pg * ple, gfin_ref[...])

    assert 2 * STRIP == LANES
    for pair in range(N_EXPERTS // 2):
        rel0, g0 = rel_gate(2 * pair)
        rel1, g1 = rel_gate(2 * pair + 1)
        target = jnp.where(left, rel0, rel1 + STRIP)
        gate = jnp.where(left, g0, g1)
        w_ref[:, pair * LANES:(pair + 1) * LANES] = jnp.where(lane_f == target, gate, 0.0).astype(BF16)
    acc_ref[i % 2] = jnp.dot(w_ref[...], buf_ref[slot], preferred_element_type=F32)

    more = functools.reduce(jnp.logical_or, [nstrip_ref[tile * N_EXPERTS + e] > 1 for e in range(N_EXPERTS)])

    @pl.when(more & (i < ntile))
    def _():
        for e in range(N_EXPERTS):
            def extra(k, carry, e=e):
                start = pl.multiple_of(start_ref[tile * N_EXPERTS + e] + k * STRIP, BF16_ROWS)
                cp = pltpu.make_async_copy(ye_ref.at[e, pl.ds(start, STRIP), :], xbuf_ref, xsem_ref.at[0])
                cp.start()
                rel, gate = rel_gate(e, STRIP)
                strip_lane = lax.broadcasted_iota(I32, (tb, STRIP), 1) + k * STRIP
                w = jnp.where(strip_lane.astype(F32) == rel, gate, 0.0).astype(BF16)
                cp.wait()
                acc_ref[i % 2] += jnp.dot(w, xbuf_ref[...], preferred_element_type=F32)
                return carry
            lax.fori_loop(1, jnp.maximum(nstrip_ref[tile * N_EXPERTS + e], 1), extra, 0)


def _combine(start, nstrip, x1, pos, gate, pe2, ye, wple, wpg, g_ple, g_pg, g_final):
    t = x1.shape[0]
    tb = TB_ROUTE
    ntile = t // tb
    full = lambda a: pl.BlockSpec(a.shape, lambda i, *_: (0,) * a.ndim)
    this = lambda i, *_: (jnp.minimum(i, ntile - 1), 0)
    prev = lambda i, *_: (jnp.maximum(i - 1, 0), 0)
    grid_spec = pltpu.PrefetchScalarGridSpec(
        num_scalar_prefetch=2,
        grid=(ntile + 1,),
        in_specs=[pl.BlockSpec((tb, D_MODEL), prev),
                  pl.BlockSpec((tb, N_EXPERTS), this),
                  pl.BlockSpec((tb, N_EXPERTS), this),
                  pl.BlockSpec((tb, PLE_DIM), prev),
                  pl.BlockSpec(memory_space=pl.ANY),
                  full(wple), full(wpg), full(g_ple), full(g_pg), full(g_final)],
        out_specs=pl.BlockSpec((tb, D_MODEL), prev),
        scratch_shapes=[pltpu.VMEM((2, tb, D_MODEL), F32),
                        pltpu.VMEM((tb, N_EXPERTS * STRIP), BF16),
                        pltpu.VMEM((2, N_EXPERTS * STRIP, D_MODEL), BF16), pltpu.SemaphoreType.DMA((2, N_EXPERTS)),
                        pltpu.VMEM((STRIP, D_MODEL), BF16), pltpu.SemaphoreType.DMA((1,))],
    )
    return pl.pallas_call(
        _combine_kernel,
        grid_spec=grid_spec,
        out_shape=jax.ShapeDtypeStruct((t, D_MODEL), F32),
        compiler_params=_cparams(("arbitrary",)),
        name="combine",
    )(start, nstrip, x1, pos, gate, pe2, ye, wple, wpg, g_ple, g_pg, g_final)


def _moe(x1, hn, aff_t, pe2, w):
    t = x1.shape[0]
    cap = max(1, CAPACITY_FACTOR * t // N_EXPERTS)
    assert cap % TM_FFN == 0 and t % TB_ROUTE == 0
    mask_t, gate_t = _select(aff_t, cap)
    nt = t // TB_ROUTE
    cnt = jnp.sum(mask_t.reshape(N_EXPERTS, nt, TB_ROUTE), axis=2).astype(I32).T
    base = jnp.cumsum(cnt, axis=0) - cnt
    xs, pos_t = _dispatch(hn, mask_t, base.reshape(-1), cnt.reshape(-1), cap)
    ye = _ffn(xs, w["wg"], w["wu"], w["wd"], cap)
    start = base // BF16_ROWS * BF16_ROWS
    nstrip = jnp.where(cnt > 0, (base - start + cnt + STRIP - 1) // STRIP, 0)
    return _combine(start.reshape(-1), nstrip.reshape(-1), x1, pos_t.T, gate_t.T, pe2, ye,
                    w["wple"], w["wpg"], w["g_ple"], w["g_pg"], w["g_final"])


def _trunk(x, pe, w):
    b, s, _ = x.shape
    t = b * s
    x2 = x.reshape(t, D_MODEL)
    qkv, z, xbc, dt, dtt = _in_proj(x2, w["g_mix"], w["wqkv"], w["wz"], w["wxbc"], w["wdt"])
    attn = _attention(qkv.reshape(b, s, 3 * ATTN_WIDTH), w["bias_tabs"])
    ssd = _ssd(xbc.reshape(b, s, CONV_DIM), z.reshape(b, s, SSD_INNER),
               dt.reshape(SSD_GROUPS, b, s, 2 * HEADS_PER_GROUP), dtt,
               w["conv_w"], w["conv_b"], w["dtb"], w["alog"], w["dsk"], w["gn"])
    x1, hn, aff_t = _out_proj(x2, attn.reshape(t, ATTN_WIDTH), ssd.reshape(t, SSD_INNER),
                              w["wo_a"], w["wo_s"], w["g_ffn"], w["wr_t"])
    y = _moe(x1, hn, aff_t, pe.reshape(t, PLE_DIM), w)
    return y.reshape(b, s, D_MODEL)


def _group_heads(a):
    parts = [jnp.concatenate([a[..., 0, g * HEADS_PER_GROUP:(g + 1) * HEADS_PER_GROUP],
                              a[..., 1, g * HEADS_PER_GROUP:(g + 1) * HEADS_PER_GROUP]], axis=-1)
             for g in range(SSD_GROUPS)]
    return jnp.stack(parts)


def _prepare(rel_bias, g_mix, w_in, conv_w, conv_b, dt_bias, a_log, d_skip, g_ssd, w_out, g_ffn, w_router,
             w_gate, w_up, w_down, g_pg, w_pg, w_ple, g_ple, g_final):
    row = lambda v: v.reshape(1, -1).astype(F32)
    c0 = 3 * ATTN_WIDTH
    c1 = c0 + SSD_INNER
    c2 = c1 + CONV_DIM
    w_dt = w_in[:, c2:].reshape(D_MODEL, 2, SSD_HEADS)
    return dict(
        g_mix=row(g_mix), wqkv=w_in[:, :c0].astype(BF16), wz=w_in[:, c0:c1].astype(BF16),
        wxbc=w_in[:, c1:c2].astype(BF16), wdt=_group_heads(w_dt).astype(BF16),
        bias_tabs=_attn_bias_tables(rel_bias),
        conv_w=conv_w.astype(F32), conv_b=row(conv_b),
        dtb=_group_heads(dt_bias.astype(F32)).reshape(SSD_GROUPS, 1, 2 * HEADS_PER_GROUP),
        alog=_group_heads(a_log.astype(F32)).reshape(SSD_GROUPS, 1, 2 * HEADS_PER_GROUP),
        dsk=row(jnp.repeat(d_skip, SSD_HEAD_DIM)), gn=row(g_ssd),
        wo_a=w_out[:ATTN_WIDTH].astype(BF16), wo_s=w_out[ATTN_WIDTH:].astype(BF16),
        g_ffn=row(g_ffn), wr_t=w_router.T.astype(F32),
        wg=w_gate.astype(BF16), wu=w_up.astype(BF16), wd=w_down.astype(BF16),
        g_pg=row(g_pg), wpg=w_pg.astype(BF16), wple=w_ple.astype(BF16), g_ple=row(g_ple), g_final=row(g_final),
    )


def kernel(x_prompt, x_sample, p_prompt, p_sample, rel_bias, g_mix, w_in, conv_w, conv_b, dt_bias, a_log, d_skip,
           g_ssd, w_out, g_ffn, w_router, w_gate, w_up, w_down, g_pg, w_pg, w_ple, g_ple, g_final):
    assert g_mix.shape[0] == 1, "single-layer trunk"
    w = _prepare(rel_bias, g_mix[0], w_in[0], conv_w[0], conv_b[0], dt_bias[0], a_log[0], d_skip[0], g_ssd[0],
                 w_out[0], g_ffn[0], w_router[0], w_gate[0], w_up[0], w_down[0], g_pg[0], w_pg[0], w_ple[0],
                 g_ple[0], g_final)
    return _trunk(x_prompt, p_prompt[0], w), _trunk(x_sample, p_sample[0], w)
```

```python
import functools
import math

import jax
import jax.numpy as jnp
import numpy as np
from jax import lax
from jax.experimental import pallas as pl
from jax.experimental.pallas import tpu as pltpu

F32 = jnp.float32
BF16 = jnp.bfloat16
I32 = jnp.int32

D_MODEL = 1024
PLE_DIM = 256
N_ATTN_HEADS = 8
ATTN_HEAD_DIM = 64
ATTN_WIDTH = N_ATTN_HEADS * ATTN_HEAD_DIM
DILATED_PATTERNS = ((128, 1), (512, 4), (2048, 16))
REL_BUCKETS = 32
REL_MAX_DISTANCE = 1024
SSD_HEADS = 8
SSD_HEAD_DIM = 64
SSD_INNER = SSD_HEADS * SSD_HEAD_DIM
SSD_GROUPS = 2
SSD_STATE = 128
SSD_CONV = 5
SSD_CHUNK = 128
CONV_DIM = SSD_INNER + 2 * SSD_GROUPS * SSD_STATE
N_EXPERTS = 16
EXPERT_FF = 2816
CAPACITY_FACTOR = 2
EPS = 1e-6

LANES = 128
SUBLANES = 8
BF16_ROWS = 16
VMEM_LIMIT = 56 * 1024 * 1024
NEG = -1e30

HEADS_PER_GROUP = SSD_HEADS // SSD_GROUPS
GROUP_COLS = SSD_INNER // SSD_GROUPS
ATT_HALF = 64
ATT_BQ = 128
ATT_KW = 256
TM_PROJ = 512
SSD_UNROLL = 4
TB_ROUTE = 256
STRIP = 64
TM_FFN = 512
SEL_CHUNK = 1024


def _cparams(sem):
    return pltpu.CompilerParams(dimension_semantics=sem, vmem_limit_bytes=VMEM_LIMIT)


def _rms(x, g):
    return x * lax.rsqrt(jnp.mean(x * x, axis=-1, keepdims=True) + EPS) * g


def _sigmoid(x):
    return 0.5 * jnp.tanh(0.5 * x) + 0.5


def _silu(x):
    return x * _sigmoid(x)


def _in_proj_kernel(x_ref, g_ref, wqkv_ref, wz_ref, wxbc_ref, wdt_ref, qkv_ref, z_ref, xbc_ref, dt_ref, dtt_ref):
    h = _rms(x_ref[...], g_ref[...]).astype(BF16)
    qkv_ref[...] = jnp.dot(h, wqkv_ref[...], preferred_element_type=F32)
    z_ref[...] = jnp.dot(h, wz_ref[...], preferred_element_type=F32)
    xbc_ref[...] = jnp.dot(h, wxbc_ref[...], preferred_element_type=F32)
    for g in range(SSD_GROUPS):
        dt = jnp.dot(h, wdt_ref[g], preferred_element_type=F32)
        dt_ref[g] = dt
        dtt_ref[g] = dt.T


def _in_proj(x2, g_mix, wqkv, wz, wxbc, wdt):
    t = x2.shape[0]
    tm = TM_PROJ
    full = lambda a: pl.BlockSpec(a.shape, lambda i: (0,) * a.ndim)
    return pl.pallas_call(
        _in_proj_kernel,
        grid=(t // tm,),
        in_specs=[pl.BlockSpec((tm, D_MODEL), lambda i: (i, 0)), full(g_mix), full(wqkv), full(wz), full(wxbc), full(wdt)],
        out_specs=[
            pl.BlockSpec((tm, 3 * ATTN_WIDTH), lambda i: (i, 0)),
            pl.BlockSpec((tm, SSD_INNER), lambda i: (i, 0)),
            pl.BlockSpec((tm, CONV_DIM), lambda i: (i, 0)),
            pl.BlockSpec((SSD_GROUPS, tm, 2 * HEADS_PER_GROUP), lambda i: (0, i, 0)),
            pl.BlockSpec((SSD_GROUPS, 2 * HEADS_PER_GROUP, tm), lambda i: (0, 0, i)),
        ],
        out_shape=[
            jax.ShapeDtypeStruct((t, 3 * ATTN_WIDTH), F32),
            jax.ShapeDtypeStruct((t, SSD_INNER), F32),
            jax.ShapeDtypeStruct((t, CONV_DIM), F32),
            jax.ShapeDtypeStruct((SSD_GROUPS, t, 2 * HEADS_PER_GROUP), F32),
            jax.ShapeDtypeStruct((SSD_GROUPS, 2 * HEADS_PER_GROUP, t), F32),
        ],
        compiler_params=_cparams(("parallel",)),
        name="in_proj",
    )(x2, g_mix, wqkv, wz, wxbc, wdt)


def _t5_bucket(rel):
    half = REL_BUCKETS // 2
    max_exact = half // 2
    n = np.abs(rel)
    large = max_exact + (np.log(np.maximum(n, 1) / max_exact) / math.log(REL_MAX_DISTANCE / max_exact) * (half - max_exact)).astype(np.int32)
    large = np.minimum(large, half - 1)
    return (np.where(rel > 0, half, 0) + np.where(n < max_exact, n, large)).astype(np.int32)


_ATT_WINDOW_OFFSETS = (0, -ATT_HALF, -(ATT_KW - ATT_BQ))


def _attn_bias_tables(rel_bias):
    buckets = []
    for window, dilation in DILATED_PATTERNS:
        assert window // (2 * dilation) == ATT_HALF
        for off in _ATT_WINDOW_OFFSETS:
            m = np.arange(ATT_KW)[None, :] + off - np.arange(ATT_BQ)[:, None]
            buckets.append(np.where(np.abs(m) <= ATT_HALF, _t5_bucket(np.clip(m, -ATT_HALF, ATT_HALF) * dilation), -1))
    bucket = jnp.asarray(np.stack(buckets).reshape(len(DILATED_PATTERNS), len(_ATT_WINDOW_OFFSETS), 1, ATT_BQ, ATT_KW))
    tab = jnp.full(bucket.shape[:2] + (N_ATTN_HEADS, ATT_BQ, ATT_KW), NEG, F32)
    for bkt in range(REL_BUCKETS):
        tab = jnp.where(bucket == bkt, rel_bias[bkt].astype(F32)[None, None, :, None, None], tab)
    return tab


def _attn_kernel(q_ref, k_ref, v_ref, bias_ref, o_ref, acc_ref, m_ref, l_ref, *, s):
    lane = lax.broadcasted_iota(I32, (ATT_BQ, LANES), 1)
    first_head = lane < ATTN_HEAD_DIM
    scale = ATTN_HEAD_DIM ** -0.5
    nblk = s // ATT_BQ

    def rows(start, size, d):
        return pl.ds(start, size) if d == 1 else pl.ds(start, size, stride=d)

    def block(p, d, f):
        n = s // d
        nq = n // ATT_BQ
        kw = min(ATT_KW, n)
        r = f // nq
        q0 = (f % nq) * ATT_BQ
        ks = jnp.clip(q0 - ATT_HALF, 0, n - kw)
        kind = (q0 - ks) // ATT_HALF
        q_rows = rows(r + d * q0, ATT_BQ, d)
        k_rows = rows(r + d * ks, kw, d)
        qb = q_ref[0, q_rows, :] * scale
        kb = k_ref[0, k_rows, :].astype(BF16)
        vb = v_ref[0, k_rows, :].astype(BF16)
        q2 = jnp.concatenate([jnp.where(first_head, qb, 0.0), jnp.where(first_head, 0.0, qb)], axis=0).astype(BF16)
        sc = lax.dot_general(q2, kb, (((1,), (1,)), ((), ())), preferred_element_type=F32)
        sc = sc + bias_ref[p, kind, :, :, :kw].reshape(2 * ATT_BQ, kw)
        mx = jnp.max(sc, axis=-1, keepdims=True)
        pe = jnp.exp(sc - mx)
        sm = jnp.sum(pe, axis=-1, keepdims=True)
        o = jnp.dot(pe.astype(BF16), vb, preferred_element_type=F32)
        acc_ref[p, q_rows, :] = jnp.where(first_head, o[:ATT_BQ], o[ATT_BQ:])
        m_ref[p, q_rows, :] = jnp.where(first_head, mx[:ATT_BQ], mx[ATT_BQ:])
        l_ref[p, q_rows, :] = jnp.where(first_head, sm[:ATT_BQ], sm[ATT_BQ:])

    for p, (_, d) in enumerate(DILATED_PATTERNS):
        for f in range(nblk):
            block(p, d, f)

    def finish(i, carry):
        sl = pl.ds(pl.multiple_of(i * ATT_BQ, ATT_BQ), ATT_BQ)
        ms = [m_ref[p, sl, :] for p in range(len(DILATED_PATTERNS))]
        mx = functools.reduce(jnp.maximum, ms)
        ws = [jnp.exp(m - mx) for m in ms]
        num = sum(w * acc_ref[p, sl, :] for p, w in enumerate(ws))
        den = sum(w * l_ref[p, sl, :] for p, w in enumerate(ws))
        o_ref[0, sl, :] = (num / den).astype(o_ref.dtype)
        return carry
    lax.fori_loop(0, nblk, finish, 0)


def _attention(qkv3, bias_tabs):
    b, s, _ = qkv3.shape
    assert s % (16 * ATT_BQ) == 0
    npairs = N_ATTN_HEADS // 2
    col = lambda base: pl.BlockSpec((1, s, LANES), lambda bi, hp: (bi, 0, base + hp))
    return pl.pallas_call(
        functools.partial(_attn_kernel, s=s),
        grid=(b, npairs),
        in_specs=[col(0), col(npairs), col(2 * npairs),
                  pl.BlockSpec((len(DILATED_PATTERNS), len(_ATT_WINDOW_OFFSETS), 2, ATT_BQ, ATT_KW), lambda bi, hp: (0, 0, hp, 0, 0))],
        out_specs=pl.BlockSpec((1, s, LANES), lambda bi, hp: (bi, 0, hp)),
        out_shape=jax.ShapeDtypeStruct((b, s, ATTN_WIDTH), BF16),
        scratch_shapes=[pltpu.VMEM((len(DILATED_PATTERNS), s, LANES), F32)] * 3,
        compiler_params=_cparams(("parallel", "parallel")),
        name="attention",
    )(qkv3, qkv3, qkv3, bias_tabs)


def _expand_heads(v):
    rows = v.shape[0]
    head = lax.broadcasted_iota(I32, (rows, GROUP_COLS), 1) // SSD_HEAD_DIM
    out = jnp.broadcast_to(v[:, 0:1], (rows, GROUP_COLS))
    for j in range(1, HEADS_PER_GROUP):
        out = jnp.where(head == j, v[:, j:j + 1], out)
    return out


def _ssd_kernel(xs_ref, b_ref, c_ref, z_ref, dt_ref, dtt_ref, wx_ref, wb_ref, wc_ref, bx_ref, bb_ref, bc_ref,
                dtb_ref, alog_ref, dtbc_ref, alogc_ref, dsk_ref, gn_ref, o_ref, xc_ref, bcv_ref, ccv_ref, y_ref, h_ref, *, s):
    L = SSD_CHUNK
    nc = s // L
    row = lax.broadcasted_iota(I32, (L, L), 0)
    colm = lax.broadcasted_iota(I32, (L, L), 1)
    lower = row >= colm
    upper = row <= colm
    tril = lower.astype(F32)
    triu = upper.astype(F32)
    head_of_lane = lax.broadcasted_iota(I32, (L, GROUP_COLS), 1) // SSD_HEAD_DIM
    a_neg = -jnp.exp(alog_ref[0])
    dt_bias = dtb_ref[0]
    a_neg_col = -jnp.exp(alogc_ref[0])
    dt_bias_col = dtbc_ref[0]

    def conv(ref, w_ref, bias_ref, c, t0):
        main = ref[0, pl.ds(t0, L), :]
        p0 = pl.multiple_of(jnp.maximum(t0 - SUBLANES, 0), SUBLANES)
        n0 = pl.multiple_of(jnp.minimum(t0 + L, s - SUBLANES), SUBLANES)
        prev = ref[0, pl.ds(p0, SUBLANES), :] * jnp.where(c > 0, 1.0, 0.0)
        nxt = ref[0, pl.ds(n0, SUBLANES), :] * jnp.where(c < nc - 1, 1.0, 0.0)
        ext = jnp.concatenate([prev, main, nxt], axis=0)
        acc = jnp.broadcast_to(bias_ref[...], main.shape)
        for k in range(SSD_CONV):
            lo = SUBLANES - SSD_CONV // 2 + k
            acc = acc + w_ref[k:k + 1, :] * ext[lo:lo + L, :]
        return _silu(acc)

    def chunk(c, direction):
        t0 = pl.multiple_of(c * L, L)
        if direction == 0:
            xs = conv(xs_ref, wx_ref, bx_ref, c, t0)
            bm = conv(b_ref, wb_ref, bb_ref, c, t0)
            cm = conv(c_ref, wc_ref, bc_ref, c, t0)
            xc_ref[pl.ds(t0, L), :] = xs
            bcv_ref[pl.ds(t0, L), :] = bm
            ccv_ref[pl.ds(t0, L), :] = cm
        else:
            xs = xc_ref[pl.ds(t0, L), :]
            bm = bcv_ref[pl.ds(t0, L), :]
            cm = ccv_ref[pl.ds(t0, L), :]
        x = dt_ref[0, 0, pl.ds(t0, L), :] + dt_bias
        dt_all = jnp.maximum(x, 0.0) + jnp.log1p(jnp.exp(-jnp.abs(x)))
        a_all = dt_all * a_neg
        lo = direction * HEADS_PER_GROUP
        dt = dt_all[:, lo:lo + HEADS_PER_GROUP]
        a = a_all[:, lo:lo + HEADS_PER_GROUP]
        cum = jnp.dot(tril if direction == 0 else triu, a, preferred_element_type=F32, precision=lax.Precision.HIGHEST)
        total = cum[L - 1:L, :] if direction == 0 else cum[0:1, :]
        xr = dtt_ref[0, :, pl.ds(t0, L)] + dt_bias_col
        a_row = (jnp.maximum(xr, 0.0) + jnp.log1p(jnp.exp(-jnp.abs(xr)))) * a_neg_col
        cum_row = jnp.dot(a_row, triu if direction == 0 else tril, preferred_element_type=F32,
                          precision=lax.Precision.HIGHEST)
        mask = lower if direction == 0 else upper
        bmb = bm.astype(BF16)
        cmb = cm.astype(BF16)
        cb = lax.dot_general(cmb, bmb, (((1,), (1,)), ((), ())), preferred_element_type=F32)
        xdt = (xs * _expand_heads(dt)).astype(BF16)
        y = jnp.zeros((L, GROUP_COLS), F32)
        for j in range(HEADS_PER_GROUP):
            colb = jnp.broadcast_to(cum[:, j:j + 1], (L, L))
            rowb = jnp.broadcast_to(cum_row[lo + j:lo + j + 1, :], (L, L))
            seg = colb - rowb
            dec = jnp.exp(jnp.where(mask, seg, NEG))
            yj = jnp.dot((cb * dec).astype(BF16), xdt, preferred_element_type=F32)
            y = jnp.where(head_of_lane == j, yj, y)
        hprev = h_ref[...]
        y = y + jnp.dot(cmb, hprev.astype(BF16), preferred_element_type=F32) * _expand_heads(jnp.exp(cum))
        xw = (xs * _expand_heads(dt * jnp.exp(total - cum))).astype(BF16)
        st = lax.dot_general(bmb, xw, (((0,), (0,)), ((), ())), preferred_element_type=F32)
        h_ref[...] = hprev * _expand_heads(jnp.exp(total)) + st
        if direction == 0:
            y_ref[pl.ds(t0, L), :] = y
        else:
            y = y_ref[pl.ds(t0, L), :] + y + dsk_ref[...] * xs
            y = y * _silu(z_ref[0, pl.ds(t0, L), :])
            o_ref[0, pl.ds(t0, L), :] = _rms(y, gn_ref[...]).astype(o_ref.dtype)

    h_ref[...] = jnp.zeros_like(h_ref)

    def fwd(c, carry):
        for u in range(SSD_UNROLL):
            chunk(c * SSD_UNROLL + u, 0)
        return carry
    lax.fori_loop(0, nc // SSD_UNROLL, fwd, 0)
    h_ref[...] = jnp.zeros_like(h_ref)

    def bwd(i, carry):
        for u in range(SSD_UNROLL):
            chunk(nc - 1 - (i * SSD_UNROLL + u), 1)
        return carry
    lax.fori_loop(0, nc // SSD_UNROLL, bwd, 0)


def _ssd(xbc3, z3, dt4, dtt4, conv_w, conv_b, dtb, alog, dsk, gn):
    b, s, _ = xbc3.shape
    g_off_b = SSD_INNER // SSD_STATE
    g_off_c = g_off_b + SSD_GROUPS
    xcol = pl.BlockSpec((1, s, GROUP_COLS), lambda bi, g: (bi, 0, g))
    bcol = pl.BlockSpec((1, s, SSD_STATE), lambda bi, g: (bi, 0, g_off_b + g))
    ccol = pl.BlockSpec((1, s, SSD_STATE), lambda bi, g: (bi, 0, g_off_c + g))
    return pl.pallas_call(
        functools.partial(_ssd_kernel, s=s),
        grid=(b, SSD_GROUPS),
        in_specs=[
            xcol, bcol, ccol,
            pl.BlockSpec((1, s, GROUP_COLS), lambda bi, g: (bi, 0, g)),
            pl.BlockSpec((1, 1, s, 2 * HEADS_PER_GROUP), lambda bi, g: (g, bi, 0, 0)),
            pl.BlockSpec((1, 2 * HEADS_PER_GROUP, s), lambda bi, g: (g, 0, bi)),
            pl.BlockSpec((SSD_CONV, GROUP_COLS), lambda bi, g: (0, g)),
            pl.BlockSpec((SSD_CONV, SSD_STATE), lambda bi, g: (0, g_off_b + g)),
            pl.BlockSpec((SSD_CONV, SSD_STATE), lambda bi, g: (0, g_off_c + g)),
            pl.BlockSpec((1, GROUP_COLS), lambda bi, g: (0, g)),
            pl.BlockSpec((1, SSD_STATE), lambda bi, g: (0, g_off_b + g)),
            pl.BlockSpec((1, SSD_STATE), lambda bi, g: (0, g_off_c + g)),
            pl.BlockSpec((1, 1, 2 * HEADS_PER_GROUP), lambda bi, g: (g, 0, 0)),
            pl.BlockSpec((1, 1, 2 * HEADS_PER_GROUP), lambda bi, g: (g, 0, 0)),
            pl.BlockSpec((1, 2 * HEADS_PER_GROUP, 1), lambda bi, g: (g, 0, 0)),
            pl.BlockSpec((1, 2 * HEADS_PER_GROUP, 1), lambda bi, g: (g, 0, 0)),
            pl.BlockSpec((1, GROUP_COLS), lambda bi, g: (0, g)),
            pl.BlockSpec((1, GROUP_COLS), lambda bi, g: (0, g)),
        ],
        out_specs=pl.BlockSpec((1, s, GROUP_COLS), lambda bi, g: (bi, 0, g)),
        out_shape=jax.ShapeDtypeStruct((b, s, SSD_INNER), BF16),
        scratch_shapes=[
            pltpu.VMEM((s, GROUP_COLS), F32), pltpu.VMEM((s, SSD_STATE), F32), pltpu.VMEM((s, SSD_STATE), F32),
            pltpu.VMEM((s, GROUP_COLS), F32), pltpu.VMEM((SSD_STATE, GROUP_COLS), F32),
        ],
        compiler_params=_cparams(("parallel", "parallel")),
        name="ssd",
    )(xbc3, xbc3, xbc3, z3, dt4, dtt4, conv_w, conv_w, conv_w, conv_b, conv_b, conv_b, dtb, alog,
      dtb.reshape(SSD_GROUPS, -1, 1), alog.reshape(SSD_GROUPS, -1, 1), dsk, gn)


def _out_proj_kernel(x_ref, attn_ref, ssd_ref, wa_ref, ws_ref, g_ref, wr_ref, x1_ref, hn_ref, aff_ref):
    x1 = x_ref[...] + jnp.dot(attn_ref[...], wa_ref[...], preferred_element_type=F32) \
        + jnp.dot(ssd_ref[...], ws_ref[...], preferred_element_type=F32)
    x1_ref[...] = x1
    hn = _rms(x1, g_ref[...])
    hn_ref[...] = hn.astype(BF16)
    logits = lax.dot_general(wr_ref[...], hn, (((1,), (1,)), ((), ())), preferred_element_type=F32,
                             precision=lax.Precision.HIGHEST)
    e = jnp.exp(logits - jnp.max(logits, axis=0, keepdims=True))
    aff_ref[...] = e / jnp.sum(e, axis=0, keepdims=True)


def _out_proj(x2, attn2, ssd2, wa, ws, g_ffn, wr_t):
    t = x2.shape[0]
    tm = TM_PROJ
    full = lambda a: pl.BlockSpec(a.shape, lambda i: (0,) * a.ndim)
    return pl.pallas_call(
        _out_proj_kernel,
        grid=(t // tm,),
        in_specs=[pl.BlockSpec((tm, D_MODEL), lambda i: (i, 0)),
                  pl.BlockSpec((tm, ATTN_WIDTH), lambda i: (i, 0)),
                  pl.BlockSpec((tm, SSD_INNER), lambda i: (i, 0)),
                  full(wa), full(ws), full(g_ffn), full(wr_t)],
        out_specs=[pl.BlockSpec((tm, D_MODEL), lambda i: (i, 0)),
                   pl.BlockSpec((tm, D_MODEL), lambda i: (i, 0)),
                   pl.BlockSpec((N_EXPERTS, tm), lambda i: (0, i))],
        out_shape=[jax.ShapeDtypeStruct((t, D_MODEL), F32),
                   jax.ShapeDtypeStruct((t, D_MODEL), BF16),
                   jax.ShapeDtypeStruct((N_EXPERTS, t), F32)],
        compiler_params=_cparams(("parallel",)),
        name="out_proj",
    )(x2, attn2, ssd2, wa, ws, g_ffn, wr_t)


def _select_kernel(aff_ref, mask_ref, gate_ref, bits_ref, *, t, cap):
    nchunk = t // SEL_CHUNK
    lane = lax.broadcasted_iota(I32, (N_EXPERTS, SEL_CHUNK), 1)

    def to_bits(i, carry):
        sl = pl.ds(pl.multiple_of(i * SEL_CHUNK, SEL_CHUNK), SEL_CHUNK)
        bits_ref[:, sl] = lax.bitcast_convert_type(aff_ref[:, sl], I32)
        return carry
    lax.fori_loop(0, nchunk, to_bits, 0)

    def count(pred):
        def body(i, acc):
            sl = pl.ds(pl.multiple_of(i * SEL_CHUNK, SEL_CHUNK), SEL_CHUNK)
            return acc + pred(bits_ref[:, sl], lane + i * SEL_CHUNK).astype(F32)
        acc = lax.fori_loop(0, nchunk, body, jnp.zeros((N_EXPERTS, SEL_CHUNK), F32))
        return jnp.sum(acc, axis=1, keepdims=True)

    thr = jnp.zeros((N_EXPERTS, 1), I32)
    for bit in range(30, -1, -1):
        cand = thr | (1 << bit)
        cnt = count(lambda b, idx, cand=cand: b >= cand)
        thr = jnp.where(cnt >= cap, cand, thr)
    need = cap - count(lambda b, idx: b > thr)
    bound = jnp.zeros((N_EXPERTS, 1), I32)
    for bit in range(t.bit_length() - 1, -1, -1):
        cand = bound | (1 << bit)
        cnt = count(lambda b, idx, cand=cand: (b == thr) & (idx < cand))
        bound = jnp.where(cnt <= need, cand, bound)

    def emit(i, carry):
        sl = pl.ds(pl.multiple_of(i * SEL_CHUNK, SEL_CHUNK), SEL_CHUNK)
        b = bits_ref[:, sl]
        sel = (b > thr) | ((b == thr) & ((lane + i * SEL_CHUNK) < bound))
        mask_ref[:, sl] = sel.astype(F32)
        gate_ref[:, sl] = jnp.where(sel, aff_ref[:, sl], 0.0)
        return carry
    lax.fori_loop(0, nchunk, emit, 0)


def _select(aff_t, cap):
    t = aff_t.shape[1]
    full = pl.BlockSpec((N_EXPERTS, t), lambda i: (0, 0))
    return pl.pallas_call(
        functools.partial(_select_kernel, t=t, cap=cap),
        grid=(1,),
        in_specs=[full],
        out_specs=[full, full],
        out_shape=[jax.ShapeDtypeStruct((N_EXPERTS, t), F32)] * 2,
        scratch_shapes=[pltpu.VMEM((N_EXPERTS, t), I32)],
        compiler_params=_cparams(("arbitrary",)),
        name="select",
    )(aff_t)


def _dispatch_kernel(base_ref, cnt_ref, hn_ref, mask_ref, xs_ref, pos_ref,
                     rank_ref, onehot_ref, buf_ref, xbuf_ref, tail_ref, sem_ref, xsem_ref):
    i = pl.program_id(0)
    last = pl.num_programs(0) - 1
    tb = hn_ref.shape[0]
    slot = i % 2

    @pl.when(i == 0)
    def _():
        tail_ref[...] = jnp.zeros_like(tail_ref)
        xbuf_ref[...] = jnp.zeros_like(xbuf_ref)
        pad = [pltpu.make_async_copy(xbuf_ref, xs_ref.at[e, pl.ds(xs_ref.shape[1] - STRIP, STRIP), :], xsem_ref.at[0])
               for e in range(N_EXPERTS)]
        for cp in pad:
            cp.start()
        for cp in pad:
            cp.wait()

    r_i = lax.broadcasted_iota(I32, (tb, tb), 0)
    c_i = lax.broadcasted_iota(I32, (tb, tb), 1)
    incl = jnp.dot(mask_ref[...].astype(BF16), (r_i <= c_i).astype(BF16), preferred_element_type=F32)
    rank_ref[...] = incl - mask_ref[...]
    strip_row = lax.broadcasted_iota(I32, (STRIP, tb), 0).astype(F32)

    def onehot(e, k):
        off = (base_ref[i * N_EXPERTS + e] % BF16_ROWS).astype(F32)
        target = rank_ref[pl.ds(e, 1), :] + off
        sel = mask_ref[pl.ds(e, 1), :] > 0.0
        return (((strip_row + k * STRIP) == target) & sel).astype(BF16)

    def build(e, carry):
        pos_ref[pl.ds(e, 1), :] = rank_ref[pl.ds(e, 1), :] + base_ref[i * N_EXPERTS + e].astype(F32)
        onehot_ref[pl.ds(pl.multiple_of(e * STRIP, STRIP), STRIP), :] = onehot(e, 0)
        return carry
    for e in range(N_EXPERTS):
        build(e, 0)
    buf_ref[slot] = jnp.dot(onehot_ref[...], hn_ref[...], preferred_element_type=F32).astype(BF16)

    def strip_copy(s_, e, start):
        return pltpu.make_async_copy(buf_ref.at[s_, pl.ds(e * STRIP, STRIP), :],
                                     xs_ref.at[e, pl.ds(start, STRIP), :], sem_ref.at[s_, e])

    @pl.when(i > 0)
    def _():
        for e in range(N_EXPERTS):
            strip_copy(1 - slot, e, 0).wait()

    def finish(e, carry):
        base = base_ref[i * N_EXPERTS + e]
        off = base % BF16_ROWS
        end = off + cnt_ref[i * N_EXPERTS + e]
        row0 = pl.multiple_of(e * STRIP, STRIP)
        head = pl.ds(row0, BF16_ROWS)
        buf_ref[slot, head, :] = buf_ref[slot, head, :] + tail_ref[e]
        group = end // BF16_ROWS * BF16_ROWS
        has_tail = end % BF16_ROWS != 0

        @pl.when(jnp.logical_not(has_tail))
        def _():
            tail_ref[e] = jnp.zeros((BF16_ROWS, D_MODEL), BF16)

        @pl.when(has_tail & (group < STRIP))
        def _():
            tail_ref[e] = buf_ref[slot, pl.ds(pl.multiple_of(row0 + group, BF16_ROWS), BF16_ROWS), :]

        def extra(k, carry):
            xbuf_ref[...] = jnp.dot(onehot(e, k), hn_ref[...], preferred_element_type=F32).astype(BF16)

            @pl.when(has_tail & (group // STRIP == k))
            def _():
                tail_ref[e] = xbuf_ref[pl.ds(pl.multiple_of(group - k * STRIP, BF16_ROWS), BF16_ROWS), :]
            start = pl.multiple_of(base - off + k * STRIP, BF16_ROWS)
            cp = pltpu.make_async_copy(xbuf_ref, xs_ref.at[e, pl.ds(start, STRIP), :], xsem_ref.at[0])
            cp.start()
            cp.wait()
            return carry
        lax.fori_loop(1, jnp.maximum((end + STRIP - 1) // STRIP, 1), extra, 0)
        return carry
    for e in range(N_EXPERTS):
        finish(e, 0)

    for e in range(N_EXPERTS):
        base = base_ref[i * N_EXPERTS + e]
        strip_copy(slot, e, pl.multiple_of(base - base % BF16_ROWS, BF16_ROWS)).start()

    @pl.when(i == last)
    def _():
        for e in range(N_EXPERTS):
            strip_copy(slot, e, 0).wait()


def _dispatch(hn, mask_t, base, cnt, cap):
    t = hn.shape[0]
    tb = TB_ROUTE
    grid_spec = pltpu.PrefetchScalarGridSpec(
        num_scalar_prefetch=2,
        grid=(t // tb,),
        in_specs=[pl.BlockSpec((tb, D_MODEL), lambda i, *_: (i, 0)),
                  pl.BlockSpec((N_EXPERTS, tb), lambda i, *_: (0, i))],
        out_specs=[pl.BlockSpec(memory_space=pl.ANY),
                   pl.BlockSpec((N_EXPERTS, tb), lambda i, *_: (0, i))],
        scratch_shapes=[pltpu.VMEM((N_EXPERTS, tb), F32),
                        pltpu.VMEM((N_EXPERTS * STRIP, tb), BF16),
                        pltpu.VMEM((2, N_EXPERTS * STRIP, D_MODEL), BF16),
                        pltpu.VMEM((STRIP, D_MODEL), BF16),
                        pltpu.VMEM((N_EXPERTS, BF16_ROWS, D_MODEL), BF16),
                        pltpu.SemaphoreType.DMA((2, N_EXPERTS)), pltpu.SemaphoreType.DMA((1,))],
    )
    return pl.pallas_call(
        _dispatch_kernel,
        grid_spec=grid_spec,
        out_shape=[jax.ShapeDtypeStruct((N_EXPERTS, cap + STRIP, D_MODEL), BF16),
                   jax.ShapeDtypeStruct((N_EXPERTS, t), F32)],
        compiler_params=_cparams(("arbitrary",)),
        name="dispatch",
    )(base, cnt, hn, mask_t)


def _ffn_kernel(xs_ref, wg_ref, wu_ref, wd_ref, ye_ref):
    pad_tile = pl.program_id(1) == pl.num_programs(1) - 1

    @pl.when(jnp.logical_not(pad_tile))
    def _():
        x = xs_ref[0]
        g = jnp.dot(x, wg_ref[0], preferred_element_type=F32)
        u = jnp.dot(x, wu_ref[0], preferred_element_type=F32)
        a = (_silu(g) * u).astype(BF16)
        ye_ref[0] = jnp.dot(a, wd_ref[0], preferred_element_type=F32).astype(ye_ref.dtype)

    @pl.when(pad_tile)
    def _():
        ye_ref[...] = jnp.zeros_like(ye_ref)


def _ffn(xs, wg, wu, wd, cap):
    ntile = cap // TM_FFN
    once = pl.Buffered(1)
    return pl.pallas_call(
        _ffn_kernel,
        grid=(N_EXPERTS, ntile + 1),
        in_specs=[pl.BlockSpec((1, TM_FFN, D_MODEL), lambda e, j: (e, jnp.minimum(j, ntile - 1), 0)),
                  pl.BlockSpec((1, D_MODEL, EXPERT_FF), lambda e, j: (e, 0, 0), pipeline_mode=once),
                  pl.BlockSpec((1, D_MODEL, EXPERT_FF), lambda e, j: (e, 0, 0), pipeline_mode=once),
                  pl.BlockSpec((1, EXPERT_FF, D_MODEL), lambda e, j: (e, 0, 0), pipeline_mode=once)],
        out_specs=pl.BlockSpec((1, TM_FFN, D_MODEL), lambda e, j: (e, j, 0)),
        out_shape=jax.ShapeDtypeStruct((N_EXPERTS, cap + TM_FFN, D_MODEL), BF16),
        compiler_params=_cparams(("parallel", "parallel")),
        name="expert_ffn",
    )(xs, wg, wu, wd)


def _combine_kernel(start_ref, nstrip_ref, x1_ref, pos_ref, gate_ref, pe_ref, ye_ref, wple_ref, wpg_ref,
                    gple_ref, gpg_ref, gfin_ref, o_ref, acc_ref, w_ref, buf_ref, sem_ref, xbuf_ref, xsem_ref):
    i = pl.program_id(0)
    ntile = pl.num_programs(0) - 1
    tile = jnp.minimum(i, ntile - 1)
    tb = x1_ref.shape[0]
    slot = tile % 2
    lane = lax.broadcasted_iota(I32, (tb, LANES), 1)
    left = lane < STRIP
    lane_f = lane.astype(F32)

    def strip_copy(s_, t_, e):
        start = pl.multiple_of(start_ref[t_ * N_EXPERTS + e], BF16_ROWS)
        return pltpu.make_async_copy(ye_ref.at[e, pl.ds(start, STRIP), :],
                                     buf_ref.at[s_, pl.ds(e * STRIP, STRIP), :], sem_ref.at[s_, e])

    @pl.when(i == 0)
    def _():
        acc_ref[1] = jnp.zeros((tb, D_MODEL), F32)
        for e in range(N_EXPERTS):
            strip_copy(0, 0, e).start()

    @pl.when(i + 1 < ntile)
    def _():
        for e in range(N_EXPERTS):
            strip_copy(1 - slot, i + 1, e).start()

    @pl.when(i < ntile)
    def _():
        for e in range(N_EXPERTS):
            strip_copy(slot, i, e).wait()

    def rel_gate(e, width=LANES):
        rel = pos_ref[:, e:e + 1] - start_ref[tile * N_EXPERTS + e].astype(F32)
        return jnp.broadcast_to(rel, (tb, width)), jnp.broadcast_to(gate_ref[:, e:e + 1], (tb, width))

    x2 = x1_ref[...] + acc_ref[(i + 1) % 2]
    ple = _rms(jnp.dot(pe_ref[...].astype(BF16), wple_ref[...], preferred_element_type=F32), gple_ref[...])
    pg = _sigmoid(jnp.dot(_rms(x2, gpg_ref[...]).astype(BF16), wpg_ref[...], preferred_element_type=F32))
    o_ref[...] = _rms(x2 + pg * ple, gfin_ref[...])

    assert 2 * STRIP == LANES
    for pair in range(N_EXPERTS // 2):
        rel0, g0 = rel_gate(2 * pair)
        rel1, g1 = rel_gate(2 * pair + 1)
        target = jnp.where(left, rel0, rel1 + STRIP)
        gate = jnp.where(left, g0, g1)
        w_ref[:, pair * LANES:(pair + 1) * LANES] = jnp.where(lane_f == target, gate, 0.0).astype(BF16)
    acc_ref[i % 2] = jnp.dot(w_ref[...], buf_ref[slot], preferred_element_type=F32)

    more = functools.reduce(jnp.logical_or, [nstrip_ref[tile * N_EXPERTS + e] > 1 for e in range(N_EXPERTS)])

    @pl.when(more & (i < ntile))
    def _():
        for e in range(N_EXPERTS):
            def extra(k, carry, e=e):
                start = pl.multiple_of(start_ref[tile * N_EXPERTS + e] + k * STRIP, BF16_ROWS)
                cp = pltpu.make_async_copy(ye_ref.at[e, pl.ds(start, STRIP), :], xbuf_ref, xsem_ref.at[0])
                cp.start()
                rel, gate = rel_gate(e, STRIP)
                strip_lane = lax.broadcasted_iota(I32, (tb, STRIP), 1) + k * STRIP
                w = jnp.where(strip_lane.astype(F32) == rel, gate, 0.0).astype(BF16)
                cp.wait()
                acc_ref[i % 2] += jnp.dot(w, xbuf_ref[...], preferred_element_type=F32)
                return carry
            lax.fori_loop(1, jnp.maximum(nstrip_ref[tile * N_EXPERTS + e], 1), extra, 0)


def _combine(start, nstrip, x1, pos, gate, pe2, ye, wple, wpg, g_ple, g_pg, g_final):
    t = x1.shape[0]
    tb = TB_ROUTE
    ntile = t // tb
    full = lambda a: pl.BlockSpec(a.shape, lambda i, *_: (0,) * a.ndim)
    this = lambda i, *_: (jnp.minimum(i, ntile - 1), 0)
    prev = lambda i, *_: (jnp.maximum(i - 1, 0), 0)
    grid_spec = pltpu.PrefetchScalarGridSpec(
        num_scalar_prefetch=2,
        grid=(ntile + 1,),
        in_specs=[pl.BlockSpec((tb, D_MODEL), prev),
                  pl.BlockSpec((tb, N_EXPERTS), this),
                  pl.BlockSpec((tb, N_EXPERTS), this),
                  pl.BlockSpec((tb, PLE_DIM), prev),
                  pl.BlockSpec(memory_space=pl.ANY),
                  full(wple), full(wpg), full(g_ple), full(g_pg), full(g_final)],
        out_specs=pl.BlockSpec((tb, D_MODEL), prev),
        scratch_shapes=[pltpu.VMEM((2, tb, D_MODEL), F32),
                        pltpu.VMEM((tb, N_EXPERTS * STRIP), BF16),
                        pltpu.VMEM((2, N_EXPERTS * STRIP, D_MODEL), BF16), pltpu.SemaphoreType.DMA((2, N_EXPERTS)),
                        pltpu.VMEM((STRIP, D_MODEL), BF16), pltpu.SemaphoreType.DMA((1,))],
    )
    return pl.pallas_call(
        _combine_kernel,
        grid_spec=grid_spec,
        out_shape=jax.ShapeDtypeStruct((t, D_MODEL), F32),
        compiler_params=_cparams(("arbitrary",)),
        name="combine",
    )(start, nstrip, x1, pos, gate, pe2, ye, wple, wpg, g_ple, g_pg, g_final)


def _moe(x1, hn, aff_t, pe2, w):
    t = x1.shape[0]
    cap = max(1, CAPACITY_FACTOR * t // N_EXPERTS)
    assert cap % TM_FFN == 0 and t % TB_ROUTE == 0
    mask_t, gate_t = _select(aff_t, cap)
    nt = t // TB_ROUTE
    cnt = jnp.sum(mask_t.reshape(N_EXPERTS, nt, TB_ROUTE), axis=2).astype(I32).T
    base = jnp.cumsum(cnt, axis=0) - cnt
    xs, pos_t = _dispatch(hn, mask_t, base.reshape(-1), cnt.reshape(-1), cap)
    ye = _ffn(xs, w["wg"], w["wu"], w["wd"], cap)
    start = base // BF16_ROWS * BF16_ROWS
    nstrip = jnp.where(cnt > 0, (base - start + cnt + STRIP - 1) // STRIP, 0)
    return _combine(start.reshape(-1), nstrip.reshape(-1), x1, pos_t.T, gate_t.T, pe2, ye,
                    w["wple"], w["wpg"], w["g_ple"], w["g_pg"], w["g_final"])


def _trunk(x, pe, w):
    b, s, _ = x.shape
    t = b * s
    x2 = x.reshape(t, D_MODEL)
    qkv, z, xbc, dt, dtt = _in_proj(x2, w["g_mix"], w["wqkv"], w["wz"], w["wxbc"], w["wdt"])
    attn = _attention(qkv.reshape(b, s, 3 * ATTN_WIDTH), w["bias_tabs"])
    ssd = _ssd(xbc.reshape(b, s, CONV_DIM), z.reshape(b, s, SSD_INNER),
               dt.reshape(SSD_GROUPS, b, s, 2 * HEADS_PER_GROUP), dtt,
               w["conv_w"], w["conv_b"], w["dtb"], w["alog"], w["dsk"], w["gn"])
    x1, hn, aff_t = _out_proj(x2, attn.reshape(t, ATTN_WIDTH), ssd.reshape(t, SSD_INNER),
                              w["wo_a"], w["wo_s"], w["g_ffn"], w["wr_t"])
    y = _moe(x1, hn, aff_t, pe.reshape(t, PLE_DIM), w)
    return y.reshape(b, s, D_MODEL)


def _group_heads(a):
    parts = [jnp.concatenate([a[..., 0, g * HEADS_PER_GROUP:(g + 1) * HEADS_PER_GROUP],
                              a[..., 1, g * HEADS_PER_GROUP:(g + 1) * HEADS_PER_GROUP]], axis=-1)
             for g in range(SSD_GROUPS)]
    return jnp.stack(parts)


def _prepare(rel_bias, g_mix, w_in, conv_w, conv_b, dt_bias, a_log, d_skip, g_ssd, w_out, g_ffn, w_router,
             w_gate, w_up, w_down, g_pg, w_pg, w_ple, g_ple, g_final):
    row = lambda v: v.reshape(1, -1).astype(F32)
    c0 = 3 * ATTN_WIDTH
    c1 = c0 + SSD_INNER
    c2 = c1 + CONV_DIM
    w_dt = w_in[:, c2:].reshape(D_MODEL, 2, SSD_HEADS)
    return dict(
        g_mix=row(g_mix), wqkv=w_in[:, :c0].astype(BF16), wz=w_in[:, c0:c1].astype(BF16),
        wxbc=w_in[:, c1:c2].astype(BF16), wdt=_group_heads(w_dt).astype(BF16),
        bias_tabs=_attn_bias_tables(rel_bias),
        conv_w=conv_w.astype(F32), conv_b=row(conv_b),
        dtb=_group_heads(dt_bias.astype(F32)).reshape(SSD_GROUPS, 1, 2 * HEADS_PER_GROUP),
        alog=_group_heads(a_log.astype(F32)).reshape(SSD_GROUPS, 1, 2 * HEADS_PER_GROUP),
        dsk=row(jnp.repeat(d_skip, SSD_HEAD_DIM)), gn=row(g_ssd),
        wo_a=w_out[:ATTN_WIDTH].astype(BF16), wo_s=w_out[ATTN_WIDTH:].astype(BF16),
        g_ffn=row(g_ffn), wr_t=w_router.T.astype(F32),
        wg=w_gate.astype(BF16), wu=w_up.astype(BF16), wd=w_down.astype(BF16),
        g_pg=row(g_pg), wpg=w_pg.astype(BF16), wple=w_ple.astype(BF16), g_ple=row(g_ple), g_final=row(g_final),
    )


def kernel(x_prompt, x_sample, p_prompt, p_sample, rel_bias, g_mix, w_in, conv_w, conv_b, dt_bias, a_log, d_skip,
           g_ssd, w_out, g_ffn, w_router, w_gate, w_up, w_down, g_pg, w_pg, w_ple, g_ple, g_final):
    assert g_mix.shape[0] == 1, "single-layer trunk"
    w = _prepare(rel_bias, g_mix[0], w_in[0], conv_w[0], conv_b[0], dt_bias[0], a_log[0], d_skip[0], g_ssd[0],
                 w_out[0], g_ffn[0], w_router[0], w_gate[0], w_up[0], w_down[0], g_pg[0], w_pg[0], w_ple[0],
                 g_ple[0], g_final)
    return _trunk(x_prompt, p_prompt[0], w), _trunk(x_sample, p_sample[0], w)
```

```python
import functools
import math

import jax
import jax.numpy as jnp
import numpy as np
from jax import lax
from jax.experimental import pallas as pl
from jax.experimental.pallas import tpu as pltpu

F32 = jnp.float32
BF16 = jnp.bfloat16
I32 = jnp.int32

D_MODEL = 1024
PLE_DIM = 256
N_ATTN_HEADS = 8
ATTN_HEAD_DIM = 64
ATTN_WIDTH = N_ATTN_HEADS * ATTN_HEAD_DIM
DILATED_PATTERNS = ((128, 1), (512, 4), (2048, 16))
REL_BUCKETS = 32
REL_MAX_DISTANCE = 1024
SSD_HEADS = 8
SSD_HEAD_DIM = 64
SSD_INNER = SSD_HEADS * SSD_HEAD_DIM
SSD_GROUPS = 2
SSD_STATE = 128
SSD_CONV = 5
SSD_CHUNK = 128
CONV_DIM = SSD_INNER + 2 * SSD_GROUPS * SSD_STATE
N_EXPERTS = 16
EXPERT_FF = 2816
CAPACITY_FACTOR = 2
EPS = 1e-6

LANES = 128
SUBLANES = 8
BF16_ROWS = 16
VMEM_LIMIT = 56 * 1024 * 1024
NEG = -1e30

HEADS_PER_GROUP = SSD_HEADS // SSD_GROUPS
GROUP_COLS = SSD_INNER // SSD_GROUPS
ATT_HALF = 64
ATT_BQ = 128
ATT_KW = 256
TM_PROJ = 1024
SSD_UNROLL = 4
TB_ROUTE = 256
STRIP = 64
TM_FFN = 512
SEL_CHUNK = 1024


def _cparams(sem):
    return pltpu.CompilerParams(dimension_semantics=sem, vmem_limit_bytes=VMEM_LIMIT)


def _rms(x, g):
    return x * lax.rsqrt(jnp.mean(x * x, axis=-1, keepdims=True) + EPS) * g


def _sigmoid(x):
    return 0.5 * jnp.tanh(0.5 * x) + 0.5


def _silu(x):
    return x * _sigmoid(x)


def _in_proj_kernel(x_ref, g_ref, wqkv_ref, wz_ref, wxbc_ref, wdt_ref, qkv_ref, z_ref, xbc_ref, dt_ref, dtt_ref):
    h = _rms(x_ref[...], g_ref[...]).astype(BF16)
    qkv_ref[...] = jnp.dot(h, wqkv_ref[...], preferred_element_type=F32)
    z_ref[...] = jnp.dot(h, wz_ref[...], preferred_element_type=F32)
    xbc_ref[...] = jnp.dot(h, wxbc_ref[...], preferred_element_type=F32)
    for g in range(SSD_GROUPS):
        dt = jnp.dot(h, wdt_ref[g], preferred_element_type=F32)
        dt_ref[g] = dt
        dtt_ref[g] = dt.T


def _in_proj(x2, g_mix, wqkv, wz, wxbc, wdt):
    t = x2.shape[0]
    tm = TM_PROJ
    full = lambda a: pl.BlockSpec(a.shape, lambda i: (0,) * a.ndim)
    return pl.pallas_call(
        _in_proj_kernel,
        grid=(t // tm,),
        in_specs=[pl.BlockSpec((tm, D_MODEL), lambda i: (i, 0)), full(g_mix), full(wqkv), full(wz), full(wxbc), full(wdt)],
        out_specs=[
            pl.BlockSpec((tm, 3 * ATTN_WIDTH), lambda i: (i, 0)),
            pl.BlockSpec((tm, SSD_INNER), lambda i: (i, 0)),
            pl.BlockSpec((tm, CONV_DIM), lambda i: (i, 0)),
            pl.BlockSpec((SSD_GROUPS, tm, 2 * HEADS_PER_GROUP), lambda i: (0, i, 0)),
            pl.BlockSpec((SSD_GROUPS, 2 * HEADS_PER_GROUP, tm), lambda i: (0, 0, i)),
        ],
        out_shape=[
            jax.ShapeDtypeStruct((t, 3 * ATTN_WIDTH), F32),
            jax.ShapeDtypeStruct((t, SSD_INNER), F32),
            jax.ShapeDtypeStruct((t, CONV_DIM), F32),
            jax.ShapeDtypeStruct((SSD_GROUPS, t, 2 * HEADS_PER_GROUP), F32),
            jax.ShapeDtypeStruct((SSD_GROUPS, 2 * HEADS_PER_GROUP, t), F32),
        ],
        compiler_params=_cparams(("parallel",)),
        name="in_proj",
    )(x2, g_mix, wqkv, wz, wxbc, wdt)


def _t5_bucket(rel):
    half = REL_BUCKETS // 2
    max_exact = half // 2
    n = np.abs(rel)
    large = max_exact + (np.log(np.maximum(n, 1) / max_exact) / math.log(REL_MAX_DISTANCE / max_exact) * (half - max_exact)).astype(np.int32)
    large = np.minimum(large, half - 1)
    return (np.where(rel > 0, half, 0) + np.where(n < max_exact, n, large)).astype(np.int32)


_ATT_WINDOW_OFFSETS = (0, -ATT_HALF, -(ATT_KW - ATT_BQ))


def _attn_bias_tables(rel_bias):
    buckets = []
    for window, dilation in DILATED_PATTERNS:
        assert window // (2 * dilation) == ATT_HALF
        for off in _ATT_WINDOW_OFFSETS:
            m = np.arange(ATT_KW)[None, :] + off - np.arange(ATT_BQ)[:, None]
            buckets.append(np.where(np.abs(m) <= ATT_HALF, _t5_bucket(np.clip(m, -ATT_HALF, ATT_HALF) * dilation), -1))
    bucket = jnp.asarray(np.stack(buckets).reshape(len(DILATED_PATTERNS), len(_ATT_WINDOW_OFFSETS), 1, ATT_BQ, ATT_KW))
    tab = jnp.full(bucket.shape[:2] + (N_ATTN_HEADS, ATT_BQ, ATT_KW), NEG, F32)
    for bkt in range(REL_BUCKETS):
        tab = jnp.where(bucket == bkt, rel_bias[bkt].astype(F32)[None, None, :, None, None], tab)
    return tab


def _attn_kernel(q_ref, k_ref, v_ref, bias_ref, o_ref, acc_ref, m_ref, l_ref, *, s):
    lane = lax.broadcasted_iota(I32, (ATT_BQ, LANES), 1)
    first_head = lane < ATTN_HEAD_DIM
    scale = ATTN_HEAD_DIM ** -0.5
    nblk = s // ATT_BQ

    def rows(start, size, d):
        return pl.ds(start, size) if d == 1 else pl.ds(start, size, stride=d)

    def block(p, d, f):
        n = s // d
        nq = n // ATT_BQ
        kw = min(ATT_KW, n)
        r = f // nq
        q0 = (f % nq) * ATT_BQ
        ks = jnp.clip(q0 - ATT_HALF, 0, n - kw)
        kind = (q0 - ks) // ATT_HALF
        q_rows = rows(r + d * q0, ATT_BQ, d)
        k_rows = rows(r + d * ks, kw, d)
        qb = q_ref[0, q_rows, :] * scale
        kb = k_ref[0, k_rows, :].astype(BF16)
        vb = v_ref[0, k_rows, :].astype(BF16)
        q2 = jnp.concatenate([jnp.where(first_head, qb, 0.0), jnp.where(first_head, 0.0, qb)], axis=0).astype(BF16)
        sc = lax.dot_general(q2, kb, (((1,), (1,)), ((), ())), preferred_element_type=F32)
        sc = sc + bias_ref[p, kind, :, :, :kw].reshape(2 * ATT_BQ, kw)
        mx = jnp.max(sc, axis=-1, keepdims=True)
        pe = jnp.exp(sc - mx)
        sm = jnp.sum(pe, axis=-1, keepdims=True)
        o = jnp.dot(pe.astype(BF16), vb, preferred_element_type=F32)
        acc_ref[p, q_rows, :] = jnp.where(first_head, o[:ATT_BQ], o[ATT_BQ:])
        m_ref[p, q_rows, :] = jnp.where(first_head, mx[:ATT_BQ], mx[ATT_BQ:])
        l_ref[p, q_rows, :] = jnp.where(first_head, sm[:ATT_BQ], sm[ATT_BQ:])

    for p, (_, d) in enumerate(DILATED_PATTERNS):
        for f in range(nblk):
            block(p, d, f)

    def finish(i, carry):
        sl = pl.ds(pl.multiple_of(i * ATT_BQ, ATT_BQ), ATT_BQ)
        ms = [m_ref[p, sl, :] for p in range(len(DILATED_PATTERNS))]
        mx = functools.reduce(jnp.maximum, ms)
        ws = [jnp.exp(m - mx) for m in ms]
        num = sum(w * acc_ref[p, sl, :] for p, w in enumerate(ws))
        den = sum(w * l_ref[p, sl, :] for p, w in enumerate(ws))
        o_ref[0, sl, :] = (num / den).astype(o_ref.dtype)
        return carry
    lax.fori_loop(0, nblk, finish, 0)


def _attention(qkv3, bias_tabs):
    b, s, _ = qkv3.shape
    assert s % (16 * ATT_BQ) == 0
    npairs = N_ATTN_HEADS // 2
    col = lambda base: pl.BlockSpec((1, s, LANES), lambda bi, hp: (bi, 0, base + hp))
    return pl.pallas_call(
        functools.partial(_attn_kernel, s=s),
        grid=(b, npairs),
        in_specs=[col(0), col(npairs), col(2 * npairs),
                  pl.BlockSpec((len(DILATED_PATTERNS), len(_ATT_WINDOW_OFFSETS), 2, ATT_BQ, ATT_KW), lambda bi, hp: (0, 0, hp, 0, 0))],
        out_specs=pl.BlockSpec((1, s, LANES), lambda bi, hp: (bi, 0, hp)),
        out_shape=jax.ShapeDtypeStruct((b, s, ATTN_WIDTH), BF16),
        scratch_shapes=[pltpu.VMEM((len(DILATED_PATTERNS), s, LANES), F32)] * 3,
        compiler_params=_cparams(("parallel", "parallel")),
        name="attention",
    )(qkv3, qkv3, qkv3, bias_tabs)


def _expand_heads(v):
    rows = v.shape[0]
    head = lax.broadcasted_iota(I32, (rows, GROUP_COLS), 1) // SSD_HEAD_DIM
    out = jnp.broadcast_to(v[:, 0:1], (rows, GROUP_COLS))
    for j in range(1, HEADS_PER_GROUP):
        out = jnp.where(head == j, v[:, j:j + 1], out)
    return out


def _ssd_kernel(xs_ref, b_ref, c_ref, z_ref, dt_ref, dtt_ref, wx_ref, wb_ref, wc_ref, bx_ref, bb_ref, bc_ref,
                dtb_ref, alog_ref, dtbc_ref, alogc_ref, dsk_ref, gn_ref, o_ref, xc_ref, bcv_ref, ccv_ref, y_ref, h_ref, *, s):
    L = SSD_CHUNK
    nc = s // L
    row = lax.broadcasted_iota(I32, (L, L), 0)
    colm = lax.broadcasted_iota(I32, (L, L), 1)
    lower = row >= colm
    upper = row <= colm
    tril = lower.astype(F32)
    triu = upper.astype(F32)
    head_of_lane = lax.broadcasted_iota(I32, (L, GROUP_COLS), 1) // SSD_HEAD_DIM
    a_neg = -jnp.exp(alog_ref[0])
    dt_bias = dtb_ref[0]
    a_neg_col = -jnp.exp(alogc_ref[0])
    dt_bias_col = dtbc_ref[0]

    def conv(ref, w_ref, bias_ref, c, t0):
        main = ref[0, pl.ds(t0, L), :]
        p0 = pl.multiple_of(jnp.maximum(t0 - SUBLANES, 0), SUBLANES)
        n0 = pl.multiple_of(jnp.minimum(t0 + L, s - SUBLANES), SUBLANES)
        prev = ref[0, pl.ds(p0, SUBLANES), :] * jnp.where(c > 0, 1.0, 0.0)
        nxt = ref[0, pl.ds(n0, SUBLANES), :] * jnp.where(c < nc - 1, 1.0, 0.0)
        ext = jnp.concatenate([prev, main, nxt], axis=0)
        acc = jnp.broadcast_to(bias_ref[...], main.shape)
        for k in range(SSD_CONV):
            lo = SUBLANES - SSD_CONV // 2 + k
            acc = acc + w_ref[k:k + 1, :] * ext[lo:lo + L, :]
        return _silu(acc)

    def chunk(c, direction):
        t0 = pl.multiple_of(c * L, L)
        if direction == 0:
            xs = conv(xs_ref, wx_ref, bx_ref, c, t0)
            bm = conv(b_ref, wb_ref, bb_ref, c, t0)
            cm = conv(c_ref, wc_ref, bc_ref, c, t0)
            xc_ref[pl.ds(t0, L), :] = xs
            bcv_ref[pl.ds(t0, L), :] = bm
            ccv_ref[pl.ds(t0, L), :] = cm
        else:
            xs = xc_ref[pl.ds(t0, L), :]
            bm = bcv_ref[pl.ds(t0, L), :]
            cm = ccv_ref[pl.ds(t0, L), :]
        x = dt_ref[0, 0, pl.ds(t0, L), :] + dt_bias
        dt_all = jnp.maximum(x, 0.0) + jnp.log1p(jnp.exp(-jnp.abs(x)))
        a_all = dt_all * a_neg
        lo = direction * HEADS_PER_GROUP
        dt = dt_all[:, lo:lo + HEADS_PER_GROUP]
        a = a_all[:, lo:lo + HEADS_PER_GROUP]
        cum = jnp.dot(tril if direction == 0 else triu, a, preferred_element_type=F32, precision=lax.Precision.HIGHEST)
        total = cum[L - 1:L, :] if direction == 0 else cum[0:1, :]
        xr = dtt_ref[0, :, pl.ds(t0, L)] + dt_bias_col
        a_row = (jnp.maximum(xr, 0.0) + jnp.log1p(jnp.exp(-jnp.abs(xr)))) * a_neg_col
        cum_row = jnp.dot(a_row, triu if direction == 0 else tril, preferred_element_type=F32,
                          precision=lax.Precision.HIGHEST)
        mask = lower if direction == 0 else upper
        bmb = bm.astype(BF16)
        cmb = cm.astype(BF16)
        cb = lax.dot_general(cmb, bmb, (((1,), (1,)), ((), ())), preferred_element_type=F32)
        xdt = (xs * _expand_heads(dt)).astype(BF16)
        y = jnp.zeros((L, GROUP_COLS), F32)
        for j in range(HEADS_PER_GROUP):
            colb = jnp.broadcast_to(cum[:, j:j + 1], (L, L))
            rowb = jnp.broadcast_to(cum_row[lo + j:lo + j + 1, :], (L, L))
            seg = colb - rowb
            dec = jnp.exp(jnp.where(mask, seg, NEG))
            yj = jnp.dot((cb * dec).astype(BF16), xdt, preferred_element_type=F32)
            y = jnp.where(head_of_lane == j, yj, y)
        hprev = h_ref[...]
        y = y + jnp.dot(cmb, hprev.astype(BF16), preferred_element_type=F32) * _expand_heads(jnp.exp(cum))
        xw = (xs * _expand_heads(dt * jnp.exp(total - cum))).astype(BF16)
        st = lax.dot_general(bmb, xw, (((0,), (0,)), ((), ())), preferred_element_type=F32)
        h_ref[...] = hprev * _expand_heads(jnp.exp(total)) + st
        if direction == 0:
            y_ref[pl.ds(t0, L), :] = y
        else:
            y = y_ref[pl.ds(t0, L), :] + y + dsk_ref[...] * xs
            y = y * _silu(z_ref[0, pl.ds(t0, L), :])
            o_ref[0, pl.ds(t0, L), :] = _rms(y, gn_ref[...]).astype(o_ref.dtype)

    h_ref[...] = jnp.zeros_like(h_ref)

    def fwd(c, carry):
        for u in range(SSD_UNROLL):
            chunk(c * SSD_UNROLL + u, 0)
        return carry
    lax.fori_loop(0, nc // SSD_UNROLL, fwd, 0)
    h_ref[...] = jnp.zeros_like(h_ref)

    def bwd(i, carry):
        for u in range(SSD_UNROLL):
            chunk(nc - 1 - (i * SSD_UNROLL + u), 1)
        return carry
    lax.fori_loop(0, nc // SSD_UNROLL, bwd, 0)


def _ssd(xbc3, z3, dt4, dtt4, conv_w, conv_b, dtb, alog, dsk, gn):
    b, s, _ = xbc3.shape
    g_off_b = SSD_INNER // SSD_STATE
    g_off_c = g_off_b + SSD_GROUPS
    xcol = pl.BlockSpec((1, s, GROUP_COLS), lambda bi, g: (bi, 0, g))
    bcol = pl.BlockSpec((1, s, SSD_STATE), lambda bi, g: (bi, 0, g_off_b + g))
    ccol = pl.BlockSpec((1, s, SSD_STATE), lambda bi, g: (bi, 0, g_off_c + g))
    return pl.pallas_call(
        functools.partial(_ssd_kernel, s=s),
        grid=(b, SSD_GROUPS),
        in_specs=[
            xcol, bcol, ccol,
            pl.BlockSpec((1, s, GROUP_COLS), lambda bi, g: (bi, 0, g)),
            pl.BlockSpec((1, 1, s, 2 * HEADS_PER_GROUP), lambda bi, g: (g, bi, 0, 0)),
            pl.BlockSpec((1, 2 * HEADS_PER_GROUP, s), lambda bi, g: (g, 0, bi)),
            pl.BlockSpec((SSD_CONV, GROUP_COLS), lambda bi, g: (0, g)),
            pl.BlockSpec((SSD_CONV, SSD_STATE), lambda bi, g: (0, g_off_b + g)),
            pl.BlockSpec((SSD_CONV, SSD_STATE), lambda bi, g: (0, g_off_c + g)),
            pl.BlockSpec((1, GROUP_COLS), lambda bi, g: (0, g)),
            pl.BlockSpec((1, SSD_STATE), lambda bi, g: (0, g_off_b + g)),
            pl.BlockSpec((1, SSD_STATE), lambda bi, g: (0, g_off_c + g)),
            pl.BlockSpec((1, 1, 2 * HEADS_PER_GROUP), lambda bi, g: (g, 0, 0)),
            pl.BlockSpec((1, 1, 2 * HEADS_PER_GROUP), lambda bi, g: (g, 0, 0)),
            pl.BlockSpec((1, 2 * HEADS_PER_GROUP, 1), lambda bi, g: (g, 0, 0)),
            pl.BlockSpec((1, 2 * HEADS_PER_GROUP, 1), lambda bi, g: (g, 0, 0)),
            pl.BlockSpec((1, GROUP_COLS), lambda bi, g: (0, g)),
            pl.BlockSpec((1, GROUP_COLS), lambda bi, g: (0, g)),
        ],
        out_specs=pl.BlockSpec((1, s, GROUP_COLS), lambda bi, g: (bi, 0, g)),
        out_shape=jax.ShapeDtypeStruct((b, s, SSD_INNER), BF16),
        scratch_shapes=[
            pltpu.VMEM((s, GROUP_COLS), F32), pltpu.VMEM((s, SSD_STATE), F32), pltpu.VMEM((s, SSD_STATE), F32),
            pltpu.VMEM((s, GROUP_COLS), F32), pltpu.VMEM((SSD_STATE, GROUP_COLS), F32),
        ],
        compiler_params=_cparams(("parallel", "parallel")),
        name="ssd",
    )(xbc3, xbc3, xbc3, z3, dt4, dtt4, conv_w, conv_w, conv_w, conv_b, conv_b, conv_b, dtb, alog,
      dtb.reshape(SSD_GROUPS, -1, 1), alog.reshape(SSD_GROUPS, -1, 1), dsk, gn)


def _out_proj_kernel(x_ref, attn_ref, ssd_ref, wa_ref, ws_ref, g_ref, wr_ref, x1_ref, hn_ref, aff_ref):
    x1 = x_ref[...] + jnp.dot(attn_ref[...], wa_ref[...], preferred_element_type=F32) \
        + jnp.dot(ssd_ref[...], ws_ref[...], preferred_element_type=F32)
    x1_ref[...] = x1
    hn = _rms(x1, g_ref[...])
    hn_ref[...] = hn.astype(BF16)
    nt_dot = lambda a, b: lax.dot_general(a, b, (((1,), (1,)), ((), ())), preferred_element_type=F32)
    h_hi = hn.astype(BF16)
    h_lo = (hn - h_hi.astype(F32)).astype(BF16)
    wr = wr_ref[...]
    w_hi = wr.astype(BF16)
    w_lo = (wr - w_hi.astype(F32)).astype(BF16)
    logits = nt_dot(w_hi, h_hi) + (nt_dot(w_lo, h_hi) + nt_dot(w_hi, h_lo))
    e = jnp.exp(logits - jnp.max(logits, axis=0, keepdims=True))
    aff_ref[...] = e / jnp.sum(e, axis=0, keepdims=True)


def _out_proj(x2, attn2, ssd2, wa, ws, g_ffn, wr_t):
    t = x2.shape[0]
    tm = TM_PROJ
    full = lambda a: pl.BlockSpec(a.shape, lambda i: (0,) * a.ndim)
    return pl.pallas_call(
        _out_proj_kernel,
        grid=(t // tm,),
        in_specs=[pl.BlockSpec((tm, D_MODEL), lambda i: (i, 0)),
                  pl.BlockSpec((tm, ATTN_WIDTH), lambda i: (i, 0)),
                  pl.BlockSpec((tm, SSD_INNER), lambda i: (i, 0)),
                  full(wa), full(ws), full(g_ffn), full(wr_t)],
        out_specs=[pl.BlockSpec((tm, D_MODEL), lambda i: (i, 0)),
                   pl.BlockSpec((tm, D_MODEL), lambda i: (i, 0)),
                   pl.BlockSpec((N_EXPERTS, tm), lambda i: (0, i))],
        out_shape=[jax.ShapeDtypeStruct((t, D_MODEL), F32),
                   jax.ShapeDtypeStruct((t, D_MODEL), BF16),
                   jax.ShapeDtypeStruct((N_EXPERTS, t), F32)],
        compiler_params=_cparams(("parallel",)),
        name="out_proj",
    )(x2, attn2, ssd2, wa, ws, g_ffn, wr_t)


def _select_kernel(aff_ref, mask_ref, gate_ref, bits_ref, *, t, cap):
    nchunk = t // SEL_CHUNK
    lane = lax.broadcasted_iota(I32, (N_EXPERTS, SEL_CHUNK), 1)

    def to_bits(i, carry):
        sl = pl.ds(pl.multiple_of(i * SEL_CHUNK, SEL_CHUNK), SEL_CHUNK)
        bits_ref[:, sl] = lax.bitcast_convert_type(aff_ref[:, sl], I32)
        return carry
    lax.fori_loop(0, nchunk, to_bits, 0)

    def count(pred):
        def body(i, acc):
            sl = pl.ds(pl.multiple_of(i * SEL_CHUNK, SEL_CHUNK), SEL_CHUNK)
            return acc + pred(bits_ref[:, sl], lane + i * SEL_CHUNK).astype(F32)
        acc = lax.fori_loop(0, nchunk, body, jnp.zeros((N_EXPERTS, SEL_CHUNK), F32))
        return jnp.sum(acc, axis=1, keepdims=True)

    thr = jnp.zeros((N_EXPERTS, 1), I32)
    for bit in range(30, -1, -1):
        cand = thr | (1 << bit)
        cnt = count(lambda b, idx, cand=cand: b >= cand)
        thr = jnp.where(cnt >= cap, cand, thr)
    need = cap - count(lambda b, idx: b > thr)
    bound = jnp.zeros((N_EXPERTS, 1), I32)
    for bit in range(t.bit_length() - 1, -1, -1):
        cand = bound | (1 << bit)
        cnt = count(lambda b, idx, cand=cand: (b == thr) & (idx < cand))
        bound = jnp.where(cnt <= need, cand, bound)

    def emit(i, carry):
        sl = pl.ds(pl.multiple_of(i * SEL_CHUNK, SEL_CHUNK), SEL_CHUNK)
        b = bits_ref[:, sl]
        sel = (b > thr) | ((b == thr) & ((lane + i * SEL_CHUNK) < bound))
        mask_ref[:, sl] = sel.astype(F32)
        gate_ref[:, sl] = jnp.where(sel, aff_ref[:, sl], 0.0)
        return carry
    lax.fori_loop(0, nchunk, emit, 0)


def _select(aff_t, cap):
    t = aff_t.shape[1]
    full = pl.BlockSpec((N_EXPERTS, t), lambda i: (0, 0))
    return pl.pallas_call(
        functools.partial(_select_kernel, t=t, cap=cap),
        grid=(1,),
        in_specs=[full],
        out_specs=[full, full],
        out_shape=[jax.ShapeDtypeStruct((N_EXPERTS, t), F32)] * 2,
        scratch_shapes=[pltpu.VMEM((N_EXPERTS, t), I32)],
        compiler_params=_cparams(("arbitrary",)),
        name="select",
    )(aff_t)


def _dispatch_kernel(base_ref, cnt_ref, hn_ref, mask_ref, xs_ref, pos_ref,
                     rank_ref, onehot_ref, buf_ref, xbuf_ref, tail_ref, sem_ref, xsem_ref):
    i = pl.program_id(0)
    last = pl.num_programs(0) - 1
    tb = hn_ref.shape[0]
    slot = i % 2

    @pl.when(i == 0)
    def _():
        tail_ref[...] = jnp.zeros_like(tail_ref)
        xbuf_ref[...] = jnp.zeros_like(xbuf_ref)
        pad = [pltpu.make_async_copy(xbuf_ref, xs_ref.at[e, pl.ds(xs_ref.shape[1] - STRIP, STRIP), :], xsem_ref.at[0])
               for e in range(N_EXPERTS)]
        for cp in pad:
            cp.start()
        for cp in pad:
            cp.wait()

    r_i = lax.broadcasted_iota(I32, (tb, tb), 0)
    c_i = lax.broadcasted_iota(I32, (tb, tb), 1)
    incl = jnp.dot(mask_ref[...].astype(BF16), (r_i <= c_i).astype(BF16), preferred_element_type=F32)
    rank_ref[...] = incl - mask_ref[...]
    strip_row = lax.broadcasted_iota(I32, (STRIP, tb), 0).astype(F32)

    def onehot(e, k):
        off = (base_ref[i * N_EXPERTS + e] % BF16_ROWS).astype(F32)
        target = rank_ref[pl.ds(e, 1), :] + off
        sel = mask_ref[pl.ds(e, 1), :] > 0.0
        return (((strip_row + k * STRIP) == target) & sel).astype(BF16)

    def build(e, carry):
        pos_ref[pl.ds(e, 1), :] = rank_ref[pl.ds(e, 1), :] + base_ref[i * N_EXPERTS + e].astype(F32)
        onehot_ref[pl.ds(pl.multiple_of(e * STRIP, STRIP), STRIP), :] = onehot(e, 0)
        return carry
    for e in range(N_EXPERTS):
        build(e, 0)
    buf_ref[slot] = jnp.dot(onehot_ref[...], hn_ref[...], preferred_element_type=F32).astype(BF16)

    def strip_copy(s_, e, start):
        return pltpu.make_async_copy(buf_ref.at[s_, pl.ds(e * STRIP, STRIP), :],
                                     xs_ref.at[e, pl.ds(start, STRIP), :], sem_ref.at[s_, e])

    @pl.when(i > 0)
    def _():
        for e in range(N_EXPERTS):
            strip_copy(1 - slot, e, 0).wait()

    def finish(e, carry):
        base = base_ref[i * N_EXPERTS + e]
        off = base % BF16_ROWS
        end = off + cnt_ref[i * N_EXPERTS + e]
        row0 = pl.multiple_of(e * STRIP, STRIP)
        head = pl.ds(row0, BF16_ROWS)
        buf_ref[slot, head, :] = buf_ref[slot, head, :] + tail_ref[e]
        group = end // BF16_ROWS * BF16_ROWS
        has_tail = end % BF16_ROWS != 0

        @pl.when(jnp.logical_not(has_tail))
        def _():
            tail_ref[e] = jnp.zeros((BF16_ROWS, D_MODEL), BF16)

        @pl.when(has_tail & (group < STRIP))
        def _():
            tail_ref[e] = buf_ref[slot, pl.ds(pl.multiple_of(row0 + group, BF16_ROWS), BF16_ROWS), :]

        def extra(k, carry):
            xbuf_ref[...] = jnp.dot(onehot(e, k), hn_ref[...], preferred_element_type=F32).astype(BF16)

            @pl.when(has_tail & (group // STRIP == k))
            def _():
                tail_ref[e] = xbuf_ref[pl.ds(pl.multiple_of(group - k * STRIP, BF16_ROWS), BF16_ROWS), :]
            start = pl.multiple_of(base - off + k * STRIP, BF16_ROWS)
            cp = pltpu.make_async_copy(xbuf_ref, xs_ref.at[e, pl.ds(start, STRIP), :], xsem_ref.at[0])
            cp.start()
            cp.wait()
            return carry
        lax.fori_loop(1, jnp.maximum((end + STRIP - 1) // STRIP, 1), extra, 0)
        return carry
    for e in range(N_EXPERTS):
        finish(e, 0)

    for e in range(N_EXPERTS):
        base = base_ref[i * N_EXPERTS + e]
        strip_copy(slot, e, pl.multiple_of(base - base % BF16_ROWS, BF16_ROWS)).start()

    @pl.when(i == last)
    def _():
        for e in range(N_EXPERTS):
            strip_copy(slot, e, 0).wait()


def _dispatch(hn, mask_t, base, cnt, cap):
    t = hn.shape[0]
    tb = TB_ROUTE
    grid_spec = pltpu.PrefetchScalarGridSpec(
        num_scalar_prefetch=2,
        grid=(t // tb,),
        in_specs=[pl.BlockSpec((tb, D_MODEL), lambda i, *_: (i, 0)),
                  pl.BlockSpec((N_EXPERTS, tb), lambda i, *_: (0, i))],
        out_specs=[pl.BlockSpec(memory_space=pl.ANY),
                   pl.BlockSpec((N_EXPERTS, tb), lambda i, *_: (0, i))],
        scratch_shapes=[pltpu.VMEM((N_EXPERTS, tb), F32),
                        pltpu.VMEM((N_EXPERTS * STRIP, tb), BF16),
                        pltpu.VMEM((2, N_EXPERTS * STRIP, D_MODEL), BF16),
                        pltpu.VMEM((STRIP, D_MODEL), BF16),
                        pltpu.VMEM((N_EXPERTS, BF16_ROWS, D_MODEL), BF16),
                        pltpu.SemaphoreType.DMA((2, N_EXPERTS)), pltpu.SemaphoreType.DMA((1,))],
    )
    return pl.pallas_call(
        _dispatch_kernel,
        grid_spec=grid_spec,
        out_shape=[jax.ShapeDtypeStruct((N_EXPERTS, cap + STRIP, D_MODEL), BF16),
                   jax.ShapeDtypeStruct((N_EXPERTS, t), F32)],
        compiler_params=_cparams(("arbitrary",)),
        name="dispatch",
    )(base, cnt, hn, mask_t)


def _ffn_kernel(xs_ref, wg_ref, wu_ref, wd_ref, ye_ref):
    pad_tile = pl.program_id(1) == pl.num_programs(1) - 1

    @pl.when(jnp.logical_not(pad_tile))
    def _():
        x = xs_ref[0]
        g = jnp.dot(x, wg_ref[0], preferred_element_type=F32)
        u = jnp.dot(x, wu_ref[0], preferred_element_type=F32)
        a = (_silu(g) * u).astype(BF16)
        ye_ref[0] = jnp.dot(a, wd_ref[0], preferred_element_type=F32).astype(ye_ref.dtype)

    @pl.when(pad_tile)
    def _():
        ye_ref[...] = jnp.zeros_like(ye_ref)


def _ffn(xs, wg, wu, wd, cap):
    ntile = cap // TM_FFN
    once = pl.Buffered(1)
    return pl.pallas_call(
        _ffn_kernel,
        grid=(N_EXPERTS, ntile + 1),
        in_specs=[pl.BlockSpec((1, TM_FFN, D_MODEL), lambda e, j: (e, jnp.minimum(j, ntile - 1), 0)),
                  pl.BlockSpec((1, D_MODEL, EXPERT_FF), lambda e, j: (e, 0, 0), pipeline_mode=once),
                  pl.BlockSpec((1, D_MODEL, EXPERT_FF), lambda e, j: (e, 0, 0), pipeline_mode=once),
                  pl.BlockSpec((1, EXPERT_FF, D_MODEL), lambda e, j: (e, 0, 0), pipeline_mode=once)],
        out_specs=pl.BlockSpec((1, TM_FFN, D_MODEL), lambda e, j: (e, j, 0)),
        out_shape=jax.ShapeDtypeStruct((N_EXPERTS, cap + TM_FFN, D_MODEL), BF16),
        compiler_params=_cparams(("parallel", "parallel")),
        name="expert_ffn",
    )(xs, wg, wu, wd)


def _combine_kernel(start_ref, nstrip_ref, x1_ref, pos_ref, gate_ref, pe_ref, ye_ref, wple_ref, wpg_ref,
                    gple_ref, gpg_ref, gfin_ref, o_ref, acc_ref, w_ref, buf_ref, sem_ref, xbuf_ref, xsem_ref):
    i = pl.program_id(0)
    ntile = pl.num_programs(0) - 1
    tile = jnp.minimum(i, ntile - 1)
    tb = x1_ref.shape[0]
    slot = tile % 2
    lane = lax.broadcasted_iota(I32, (tb, LANES), 1)
    left = lane < STRIP
    lane_f = lane.astype(F32)

    def strip_copy(s_, t_, e):
        start = pl.multiple_of(start_ref[t_ * N_EXPERTS + e], BF16_ROWS)
        return pltpu.make_async_copy(ye_ref.at[e, pl.ds(start, STRIP), :],
                                     buf_ref.at[s_, pl.ds(e * STRIP, STRIP), :], sem_ref.at[s_, e])

    @pl.when(i == 0)
    def _():
        acc_ref[1] = jnp.zeros((tb, D_MODEL), F32)
        for e in range(N_EXPERTS):
            strip_copy(0, 0, e).start()

    @pl.when(i + 1 < ntile)
    def _():
        for e in range(N_EXPERTS):
            strip_copy(1 - slot, i + 1, e).start()

    @pl.when(i < ntile)
    def _():
        for e in range(N_EXPERTS):
            strip_copy(slot, i, e).wait()

    def rel_gate(e, width=LANES):
        rel = pos_ref[:, e:e + 1] - start_ref[tile * N_EXPERTS + e].astype(F32)
        return jnp.broadcast_to(rel, (tb, width)), jnp.broadcast_to(gate_ref[:, e:e + 1], (tb, width))

    x2 = x1_ref[...] + acc_ref[(i + 1) % 2]
    ple = _rms(jnp.dot(pe_ref[...].astype(BF16), wple_ref[...], preferred_element_type=F32), gple_ref[...])
    pg = _sigmoid(jnp.dot(_rms(x2, gpg_ref[...]).astype(BF16), wpg_ref[...], preferred_element_type=F32))
    o_ref[...] = _rms(x2 + pg * ple, gfin_ref[...])

    assert 2 * STRIP == LANES
    for pair in range(N_EXPERTS // 2):
        rel0, g0 = rel_gate(2 * pair)
        rel1, g1 = rel_gate(2 * pair + 1)
        target = jnp.where(left, rel0, rel1 + STRIP)
        gate = jnp.where(left, g0, g1)
        w_ref[:, pair * LANES:(pair + 1) * LANES] = jnp.where(lane_f == target, gate, 0.0).astype(BF16)
    acc_ref[i % 2] = jnp.dot(w_ref[...], buf_ref[slot], preferred_element_type=F32)

    more = functools.reduce(jnp.logical_or, [nstrip_ref[tile * N_EXPERTS + e] > 1 for e in range(N_EXPERTS)])

    @pl.when(more & (i < ntile))
    def _():
        for e in range(N_EXPERTS):
            def extra(k, carry, e=e):
                start = pl.multiple_of(start_ref[tile * N_EXPERTS + e] + k * STRIP, BF16_ROWS)
                cp = pltpu.make_async_copy(ye_ref.at[e, pl.ds(start, STRIP), :], xbuf_ref, xsem_ref.at[0])
                cp.start()
                rel, gate = rel_gate(e, STRIP)
                strip_lane = lax.broadcasted_iota(I32, (tb, STRIP), 1) + k * STRIP
                w = jnp.where(strip_lane.astype(F32) == rel, gate, 0.0).astype(BF16)
                cp.wait()
                acc_ref[i % 2] += jnp.dot(w, xbuf_ref[...], preferred_element_type=F32)
                return carry
            lax.fori_loop(1, jnp.maximum(nstrip_ref[tile * N_EXPERTS + e], 1), extra, 0)


def _combine(start, nstrip, x1, pos, gate, pe2, ye, wple, wpg, g_ple, g_pg, g_final):
    t = x1.shape[0]
    tb = TB_ROUTE
    ntile = t // tb
    full = lambda a: pl.BlockSpec(a.shape, lambda i, *_: (0,) * a.ndim)
    this = lambda i, *_: (jnp.minimum(i, ntile - 1), 0)
    prev = lambda i, *_: (jnp.maximum(i - 1, 0), 0)
    grid_spec = pltpu.PrefetchScalarGridSpec(
        num_scalar_prefetch=2,
        grid=(ntile + 1,),
        in_specs=[pl.BlockSpec((tb, D_MODEL), prev),
                  pl.BlockSpec((tb, N_EXPERTS), this),
                  pl.BlockSpec((tb, N_EXPERTS), this),
                  pl.BlockSpec((tb, PLE_DIM), prev),
                  pl.BlockSpec(memory_space=pl.ANY),
                  full(wple), full(wpg), full(g_ple), full(g_pg), full(g_final)],
        out_specs=pl.BlockSpec((tb, D_MODEL), prev),
        scratch_shapes=[pltpu.VMEM((2, tb, D_MODEL), F32),
                        pltpu.VMEM((tb, N_EXPERTS * STRIP), BF16),
                        pltpu.VMEM((2, N_EXPERTS * STRIP, D_MODEL), BF16), pltpu.SemaphoreType.DMA((2, N_EXPERTS)),
                        pltpu.VMEM((STRIP, D_MODEL), BF16), pltpu.SemaphoreType.DMA((1,))],
    )
    return pl.pallas_call(
        _combine_kernel,
        grid_spec=grid_spec,
        out_shape=jax.ShapeDtypeStruct((t, D_MODEL), F32),
        compiler_params=_cparams(("arbitrary",)),
        name="combine",
    )(start, nstrip, x1, pos, gate, pe2, ye, wple, wpg, g_ple, g_pg, g_final)


def _moe(x1, hn, aff_t, pe2, w):
    t = x1.shape[0]
    cap = max(1, CAPACITY_FACTOR * t // N_EXPERTS)
    assert cap % TM_FFN == 0 and t % TB_ROUTE == 0
    mask_t, gate_t = _select(aff_t, cap)
    nt = t // TB_ROUTE
    cnt = jnp.sum(mask_t.reshape(N_EXPERTS, nt, TB_ROUTE), axis=2).astype(I32).T
    base = jnp.cumsum(cnt, axis=0) - cnt
    xs, pos_t = _dispatch(hn, mask_t, base.reshape(-1), cnt.reshape(-1), cap)
    ye = _ffn(xs, w["wg"], w["wu"], w["wd"], cap)
    start = base // BF16_ROWS * BF16_ROWS
    nstrip = jnp.where(cnt > 0, (base - start + cnt + STRIP - 1) // STRIP, 0)
    return _combine(start.reshape(-1), nstrip.reshape(-1), x1, pos_t.T, gate_t.T, pe2, ye,
                    w["wple"], w["wpg"], w["g_ple"], w["g_pg"], w["g_final"])


def _trunk(x, pe, w):
    b, s, _ = x.shape
    t = b * s
    x2 = x.reshape(t, D_MODEL)
    qkv, z, xbc, dt, dtt = _in_proj(x2, w["g_mix"], w["wqkv"], w["wz"], w["wxbc"], w["wdt"])
    attn = _attention(qkv.reshape(b, s, 3 * ATTN_WIDTH), w["bias_tabs"])
    ssd = _ssd(xbc.reshape(b, s, CONV_DIM), z.reshape(b, s, SSD_INNER),
               dt.reshape(SSD_GROUPS, b, s, 2 * HEADS_PER_GROUP), dtt,
               w["conv_w"], w["conv_b"], w["dtb"], w["alog"], w["dsk"], w["gn"])
    x1, hn, aff_t = _out_proj(x2, attn.reshape(t, ATTN_WIDTH), ssd.reshape(t, SSD_INNER),
                              w["wo_a"], w["wo_s"], w["g_ffn"], w["wr_t"])
    y = _moe(x1, hn, aff_t, pe.reshape(t, PLE_DIM), w)
    return y.reshape(b, s, D_MODEL)


def _group_heads(a):
    parts = [jnp.concatenate([a[..., 0, g * HEADS_PER_GROUP:(g + 1) * HEADS_PER_GROUP],
                              a[..., 1, g * HEADS_PER_GROUP:(g + 1) * HEADS_PER_GROUP]], axis=-1)
             for g in range(SSD_GROUPS)]
    return jnp.stack(parts)


def _prepare(rel_bias, g_mix, w_in, conv_w, conv_b, dt_bias, a_log, d_skip, g_ssd, w_out, g_ffn, w_router,
             w_gate, w_up, w_down, g_pg, w_pg, w_ple, g_ple, g_final):
    row = lambda v: v.reshape(1, -1).astype(F32)
    c0 = 3 * ATTN_WIDTH
    c1 = c0 + SSD_INNER
    c2 = c1 + CONV_DIM
    w_dt = w_in[:, c2:].reshape(D_MODEL, 2, SSD_HEADS)
    return dict(
        g_mix=row(g_mix), wqkv=w_in[:, :c0].astype(BF16), wz=w_in[:, c0:c1].astype(BF16),
        wxbc=w_in[:, c1:c2].astype(BF16), wdt=_group_heads(w_dt).astype(BF16),
        bias_tabs=_attn_bias_tables(rel_bias),
        conv_w=conv_w.astype(F32), conv_b=row(conv_b),
        dtb=_group_heads(dt_bias.astype(F32)).reshape(SSD_GROUPS, 1, 2 * HEADS_PER_GROUP),
        alog=_group_heads(a_log.astype(F32)).reshape(SSD_GROUPS, 1, 2 * HEADS_PER_GROUP),
        dsk=row(jnp.repeat(d_skip, SSD_HEAD_DIM)), gn=row(g_ssd),
        wo_a=w_out[:ATTN_WIDTH].astype(BF16), wo_s=w_out[ATTN_WIDTH:].astype(BF16),
        g_ffn=row(g_ffn), wr_t=w_router.T.astype(F32),
        wg=w_gate.astype(BF16), wu=w_up.astype(BF16), wd=w_down.astype(BF16),
        g_pg=row(g_pg), wpg=w_pg.astype(BF16), wple=w_ple.astype(BF16), g_ple=row(g_ple), g_final=row(g_final),
    )


def kernel(x_prompt, x_sample, p_prompt, p_sample, rel_bias, g_mix, w_in, conv_w, conv_b, dt_bias, a_log, d_skip,
           g_ssd, w_out, g_ffn, w_router, w_gate, w_up, w_down, g_pg, w_pg, w_ple, g_ple, g_final):
    assert g_mix.shape[0] == 1, "single-layer trunk"
    w = _prepare(rel_bias, g_mix[0], w_in[0], conv_w[0], conv_b[0], dt_bias[0], a_log[0], d_skip[0], g_ssd[0],
                 w_out[0], g_ffn[0], w_router[0], w_gate[0], w_up[0], w_down[0], g_pg[0], w_pg[0], w_ple[0],
                 g_ple[0], g_final)
    return _trunk(x_prompt, p_prompt[0], w), _trunk(x_sample, p_sample[0], w)
```

```python
import functools
import math

import jax
import jax.numpy as jnp
import numpy as np
from jax import lax
from jax.experimental import pallas as pl
from jax.experimental.pallas import tpu as pltpu

F32 = jnp.float32
BF16 = jnp.bfloat16
I32 = jnp.int32

D_MODEL = 1024
PLE_DIM = 256
N_ATTN_HEADS = 8
ATTN_HEAD_DIM = 64
ATTN_WIDTH = N_ATTN_HEADS * ATTN_HEAD_DIM
DILATED_PATTERNS = ((128, 1), (512, 4), (2048, 16))
REL_BUCKETS = 32
REL_MAX_DISTANCE = 1024
SSD_HEADS = 8
SSD_HEAD_DIM = 64
SSD_INNER = SSD_HEADS * SSD_HEAD_DIM
SSD_GROUPS = 2
SSD_STATE = 128
SSD_CONV = 5
SSD_CHUNK = 128
CONV_DIM = SSD_INNER + 2 * SSD_GROUPS * SSD_STATE
N_EXPERTS = 16
EXPERT_FF = 2816
CAPACITY_FACTOR = 2
EPS = 1e-6

LANES = 128
SUBLANES = 8
BF16_ROWS = 16
VMEM_LIMIT = 56 * 1024 * 1024
NEG = -1e30

HEADS_PER_GROUP = SSD_HEADS // SSD_GROUPS
GROUP_COLS = SSD_INNER // SSD_GROUPS
ATT_HALF = 64
ATT_BQ = 128
ATT_KW = 256
TM_PROJ = 1024
SSD_UNROLL = 4
TB_ROUTE = 256
STRIP = 64
TM_FFN = 512
SEL_CHUNK = 1024


def _cparams(sem):
    return pltpu.CompilerParams(dimension_semantics=sem, vmem_limit_bytes=VMEM_LIMIT)


def _rms(x, g):
    return x * lax.rsqrt(jnp.mean(x * x, axis=-1, keepdims=True) + EPS) * g


def _sigmoid(x):
    return 0.5 * jnp.tanh(0.5 * x) + 0.5


def _silu(x):
    return x * _sigmoid(x)


def _in_proj_kernel(x_ref, g_ref, wqkv_ref, wz_ref, wxbc_ref, wdt_ref, qkv_ref, z_ref, xbc_ref, dt_ref, dtt_ref):
    h = _rms(x_ref[...], g_ref[...]).astype(BF16)
    qkv_ref[...] = jnp.dot(h, wqkv_ref[...], preferred_element_type=F32)
    z_ref[...] = jnp.dot(h, wz_ref[...], preferred_element_type=F32)
    xbc_ref[...] = jnp.dot(h, wxbc_ref[...], preferred_element_type=F32)
    for g in range(SSD_GROUPS):
        dt = jnp.dot(h, wdt_ref[g], preferred_element_type=F32)
        dt_ref[g] = dt
        dtt_ref[g] = dt.T


def _in_proj(x2, g_mix, wqkv, wz, wxbc, wdt):
    t = x2.shape[0]
    tm = TM_PROJ
    full = lambda a: pl.BlockSpec(a.shape, lambda i: (0,) * a.ndim)
    return pl.pallas_call(
        _in_proj_kernel,
        grid=(t // tm,),
        in_specs=[pl.BlockSpec((tm, D_MODEL), lambda i: (i, 0)), full(g_mix), full(wqkv), full(wz), full(wxbc), full(wdt)],
        out_specs=[
            pl.BlockSpec((tm, 3 * ATTN_WIDTH), lambda i: (i, 0)),
            pl.BlockSpec((tm, SSD_INNER), lambda i: (i, 0)),
            pl.BlockSpec((tm, CONV_DIM), lambda i: (i, 0)),
            pl.BlockSpec((SSD_GROUPS, tm, 2 * HEADS_PER_GROUP), lambda i: (0, i, 0)),
            pl.BlockSpec((SSD_GROUPS, 2 * HEADS_PER_GROUP, tm), lambda i: (0, 0, i)),
        ],
        out_shape=[
            jax.ShapeDtypeStruct((t, 3 * ATTN_WIDTH), F32),
            jax.ShapeDtypeStruct((t, SSD_INNER), F32),
            jax.ShapeDtypeStruct((t, CONV_DIM), F32),
            jax.ShapeDtypeStruct((SSD_GROUPS, t, 2 * HEADS_PER_GROUP), F32),
            jax.ShapeDtypeStruct((SSD_GROUPS, 2 * HEADS_PER_GROUP, t), F32),
        ],
        compiler_params=_cparams(("parallel",)),
        name="in_proj",
    )(x2, g_mix, wqkv, wz, wxbc, wdt)


def _t5_bucket(rel):
    half = REL_BUCKETS // 2
    max_exact = half // 2
    n = np.abs(rel)
    large = max_exact + (np.log(np.maximum(n, 1) / max_exact) / math.log(REL_MAX_DISTANCE / max_exact) * (half - max_exact)).astype(np.int32)
    large = np.minimum(large, half - 1)
    return (np.where(rel > 0, half, 0) + np.where(n < max_exact, n, large)).astype(np.int32)


_ATT_WINDOW_OFFSETS = (0, -ATT_HALF, -(ATT_KW - ATT_BQ))


def _attn_bias_tables(rel_bias):
    buckets = []
    for window, dilation in DILATED_PATTERNS:
        assert window // (2 * dilation) == ATT_HALF
        for off in _ATT_WINDOW_OFFSETS:
            m = np.arange(ATT_KW)[None, :] + off - np.arange(ATT_BQ)[:, None]
            buckets.append(np.where(np.abs(m) <= ATT_HALF, _t5_bucket(np.clip(m, -ATT_HALF, ATT_HALF) * dilation), -1))
    bucket = jnp.asarray(np.stack(buckets).reshape(len(DILATED_PATTERNS), len(_ATT_WINDOW_OFFSETS), 1, ATT_BQ, ATT_KW))
    tab = jnp.full(bucket.shape[:2] + (N_ATTN_HEADS, ATT_BQ, ATT_KW), NEG, F32)
    for bkt in range(REL_BUCKETS):
        tab = jnp.where(bucket == bkt, rel_bias[bkt].astype(F32)[None, None, :, None, None], tab)
    return tab


def _attn_kernel(q_ref, k_ref, v_ref, bias_ref, o_ref, acc_ref, m_ref, l_ref, *, s):
    lane = lax.broadcasted_iota(I32, (ATT_BQ, LANES), 1)
    first_head = lane < ATTN_HEAD_DIM
    scale = ATTN_HEAD_DIM ** -0.5
    nblk = s // ATT_BQ

    def rows(start, size, d):
        return pl.ds(start, size) if d == 1 else pl.ds(start, size, stride=d)

    def block(p, d, f):
        n = s // d
        nq = n // ATT_BQ
        kw = min(ATT_KW, n)
        r = f // nq
        q0 = (f % nq) * ATT_BQ
        ks = jnp.clip(q0 - ATT_HALF, 0, n - kw)
        kind = (q0 - ks) // ATT_HALF
        q_rows = rows(r + d * q0, ATT_BQ, d)
        k_rows = rows(r + d * ks, kw, d)
        qb = q_ref[0, q_rows, :] * scale
        kb = k_ref[0, k_rows, :].astype(BF16)
        vb = v_ref[0, k_rows, :].astype(BF16)
        q2 = jnp.concatenate([jnp.where(first_head, qb, 0.0), jnp.where(first_head, 0.0, qb)], axis=0).astype(BF16)
        sc = lax.dot_general(q2, kb, (((1,), (1,)), ((), ())), preferred_element_type=F32)
        sc = sc + bias_ref[p, kind, :, :, :kw].reshape(2 * ATT_BQ, kw)
        mx = jnp.max(sc, axis=-1, keepdims=True)
        pe = jnp.exp(sc - mx)
        sm = jnp.sum(pe, axis=-1, keepdims=True)
        o = jnp.dot(pe.astype(BF16), vb, preferred_element_type=F32)
        acc_ref[p, q_rows, :] = jnp.where(first_head, o[:ATT_BQ], o[ATT_BQ:])
        m_ref[p, q_rows, :] = jnp.where(first_head, mx[:ATT_BQ], mx[ATT_BQ:])
        l_ref[p, q_rows, :] = jnp.where(first_head, sm[:ATT_BQ], sm[ATT_BQ:])

    for p, (_, d) in enumerate(DILATED_PATTERNS):
        for f in range(nblk):
            block(p, d, f)

    def finish(i, carry):
        sl = pl.ds(pl.multiple_of(i * ATT_BQ, ATT_BQ), ATT_BQ)
        ms = [m_ref[p, sl, :] for p in range(len(DILATED_PATTERNS))]
        mx = functools.reduce(jnp.maximum, ms)
        ws = [jnp.exp(m - mx) for m in ms]
        num = sum(w * acc_ref[p, sl, :] for p, w in enumerate(ws))
        den = sum(w * l_ref[p, sl, :] for p, w in enumerate(ws))
        o_ref[0, sl, :] = (num / den).astype(o_ref.dtype)
        return carry
    lax.fori_loop(0, nblk, finish, 0)


def _attention(qkv3, bias_tabs):
    b, s, _ = qkv3.shape
    assert s % (16 * ATT_BQ) == 0
    npairs = N_ATTN_HEADS // 2
    col = lambda base: pl.BlockSpec((1, s, LANES), lambda bi, hp: (bi, 0, base + hp))
    return pl.pallas_call(
        functools.partial(_attn_kernel, s=s),
        grid=(b, npairs),
        in_specs=[col(0), col(npairs), col(2 * npairs),
                  pl.BlockSpec((len(DILATED_PATTERNS), len(_ATT_WINDOW_OFFSETS), 2, ATT_BQ, ATT_KW), lambda bi, hp: (0, 0, hp, 0, 0))],
        out_specs=pl.BlockSpec((1, s, LANES), lambda bi, hp: (bi, 0, hp)),
        out_shape=jax.ShapeDtypeStruct((b, s, ATTN_WIDTH), BF16),
        scratch_shapes=[pltpu.VMEM((len(DILATED_PATTERNS), s, LANES), F32)] * 3,
        compiler_params=_cparams(("parallel", "parallel")),
        name="attention",
    )(qkv3, qkv3, qkv3, bias_tabs)


def _expand_heads(v):
    rows = v.shape[0]
    head = lax.broadcasted_iota(I32, (rows, GROUP_COLS), 1) // SSD_HEAD_DIM
    out = jnp.broadcast_to(v[:, 0:1], (rows, GROUP_COLS))
    for j in range(1, HEADS_PER_GROUP):
        out = jnp.where(head == j, v[:, j:j + 1], out)
    return out


def _ssd_kernel(xs_ref, b_ref, c_ref, z_ref, dt_ref, dtt_ref, wx_ref, wb_ref, wc_ref, bx_ref, bb_ref, bc_ref,
                dtb_ref, alog_ref, dtbc_ref, alogc_ref, dsk_ref, gn_ref, o_ref, xc_ref, bcv_ref, ccv_ref, y_ref, h_ref, *, s):
    L = SSD_CHUNK
    nc = s // L
    row = lax.broadcasted_iota(I32, (L, L), 0)
    colm = lax.broadcasted_iota(I32, (L, L), 1)
    lower = row >= colm
    upper = row <= colm
    tril_b = lower.astype(BF16)
    triu_b = upper.astype(BF16)

    def exact_dot(ones_mat, v, v_on_left):
        out = None
        rest = v
        for _ in range(3):
            term = rest.astype(BF16)
            rest = rest - term.astype(F32)
            part = (jnp.dot(term, ones_mat, preferred_element_type=F32) if v_on_left
                    else jnp.dot(ones_mat, term, preferred_element_type=F32))
            out = part if out is None else out + part
        return out
    head_of_lane = lax.broadcasted_iota(I32, (L, GROUP_COLS), 1) // SSD_HEAD_DIM
    a_neg = -jnp.exp(alog_ref[0])
    dt_bias = dtb_ref[0]
    a_neg_col = -jnp.exp(alogc_ref[0])
    dt_bias_col = dtbc_ref[0]

    def conv(ref, w_ref, bias_ref, c, t0):
        main = ref[0, pl.ds(t0, L), :]
        p0 = pl.multiple_of(jnp.maximum(t0 - SUBLANES, 0), SUBLANES)
        n0 = pl.multiple_of(jnp.minimum(t0 + L, s - SUBLANES), SUBLANES)
        prev = ref[0, pl.ds(p0, SUBLANES), :] * jnp.where(c > 0, 1.0, 0.0)
        nxt = ref[0, pl.ds(n0, SUBLANES), :] * jnp.where(c < nc - 1, 1.0, 0.0)
        ext = jnp.concatenate([prev, main, nxt], axis=0)
        acc = jnp.broadcast_to(bias_ref[...], main.shape)
        for k in range(SSD_CONV):
            lo = SUBLANES - SSD_CONV // 2 + k
            acc = acc + w_ref[k:k + 1, :] * ext[lo:lo + L, :]
        return _silu(acc)

    def chunk(c, direction):
        t0 = pl.multiple_of(c * L, L)
        if direction == 0:
            xs = conv(xs_ref, wx_ref, bx_ref, c, t0)
            bm = conv(b_ref, wb_ref, bb_ref, c, t0)
            cm = conv(c_ref, wc_ref, bc_ref, c, t0)
            xc_ref[pl.ds(t0, L), :] = xs
            bcv_ref[pl.ds(t0, L), :] = bm
            ccv_ref[pl.ds(t0, L), :] = cm
        else:
            xs = xc_ref[pl.ds(t0, L), :]
            bm = bcv_ref[pl.ds(t0, L), :]
            cm = ccv_ref[pl.ds(t0, L), :]
        x = dt_ref[0, 0, pl.ds(t0, L), :] + dt_bias
        dt_all = jnp.maximum(x, 0.0) + jnp.log1p(jnp.exp(-jnp.abs(x)))
        a_all = dt_all * a_neg
        lo = direction * HEADS_PER_GROUP
        dt = dt_all[:, lo:lo + HEADS_PER_GROUP]
        a = a_all[:, lo:lo + HEADS_PER_GROUP]
        cum = exact_dot(tril_b if direction == 0 else triu_b, a, False)
        total = cum[L - 1:L, :] if direction == 0 else cum[0:1, :]
        xr = dtt_ref[0, :, pl.ds(t0, L)] + dt_bias_col
        a_row = (jnp.maximum(xr, 0.0) + jnp.log1p(jnp.exp(-jnp.abs(xr)))) * a_neg_col
        cum_row = exact_dot(triu_b if direction == 0 else tril_b, a_row, True)
        mask = lower if direction == 0 else upper
        bmb = bm.astype(BF16)
        cmb = cm.astype(BF16)
        cb = lax.dot_general(cmb, bmb, (((1,), (1,)), ((), ())), preferred_element_type=F32)
        xdt = (xs * _expand_heads(dt)).astype(BF16)
        y = jnp.zeros((L, GROUP_COLS), F32)
        for j in range(HEADS_PER_GROUP):
            colb = jnp.broadcast_to(cum[:, j:j + 1], (L, L))
            rowb = jnp.broadcast_to(cum_row[lo + j:lo + j + 1, :], (L, L))
            seg = colb - rowb
            dec = jnp.exp(jnp.where(mask, seg, NEG))
            yj = jnp.dot((cb * dec).astype(BF16), xdt, preferred_element_type=F32)
            y = jnp.where(head_of_lane == j, yj, y)
        hprev = h_ref[...]
        y = y + jnp.dot(cmb, hprev.astype(BF16), preferred_element_type=F32) * _expand_heads(jnp.exp(cum))
        xw = (xs * _expand_heads(dt * jnp.exp(total - cum))).astype(BF16)
        st = lax.dot_general(bmb, xw, (((0,), (0,)), ((), ())), preferred_element_type=F32)
        h_ref[...] = hprev * _expand_heads(jnp.exp(total)) + st
        if direction == 0:
            y_ref[pl.ds(t0, L), :] = y
        else:
            y = y_ref[pl.ds(t0, L), :] + y + dsk_ref[...] * xs
            y = y * _silu(z_ref[0, pl.ds(t0, L), :])
            o_ref[0, pl.ds(t0, L), :] = _rms(y, gn_ref[...]).astype(o_ref.dtype)

    h_ref[...] = jnp.zeros_like(h_ref)

    def fwd(c, carry):
        for u in range(SSD_UNROLL):
            chunk(c * SSD_UNROLL + u, 0)
        return carry
    lax.fori_loop(0, nc // SSD_UNROLL, fwd, 0)
    h_ref[...] = jnp.zeros_like(h_ref)

    def bwd(i, carry):
        for u in range(SSD_UNROLL):
            chunk(nc - 1 - (i * SSD_UNROLL + u), 1)
        return carry
    lax.fori_loop(0, nc // SSD_UNROLL, bwd, 0)


def _ssd(xbc3, z3, dt4, dtt4, conv_w, conv_b, dtb, alog, dsk, gn):
    b, s, _ = xbc3.shape
    g_off_b = SSD_INNER // SSD_STATE
    g_off_c = g_off_b + SSD_GROUPS
    xcol = pl.BlockSpec((1, s, GROUP_COLS), lambda bi, g: (bi, 0, g))
    bcol = pl.BlockSpec((1, s, SSD_STATE), lambda bi, g: (bi, 0, g_off_b + g))
    ccol = pl.BlockSpec((1, s, SSD_STATE), lambda bi, g: (bi, 0, g_off_c + g))
    return pl.pallas_call(
        functools.partial(_ssd_kernel, s=s),
        grid=(b, SSD_GROUPS),
        in_specs=[
            xcol, bcol, ccol,
            pl.BlockSpec((1, s, GROUP_COLS), lambda bi, g: (bi, 0, g)),
            pl.BlockSpec((1, 1, s, 2 * HEADS_PER_GROUP), lambda bi, g: (g, bi, 0, 0)),
            pl.BlockSpec((1, 2 * HEADS_PER_GROUP, s), lambda bi, g: (g, 0, bi)),
            pl.BlockSpec((SSD_CONV, GROUP_COLS), lambda bi, g: (0, g)),
            pl.BlockSpec((SSD_CONV, SSD_STATE), lambda bi, g: (0, g_off_b + g)),
            pl.BlockSpec((SSD_CONV, SSD_STATE), lambda bi, g: (0, g_off_c + g)),
            pl.BlockSpec((1, GROUP_COLS), lambda bi, g: (0, g)),
            pl.BlockSpec((1, SSD_STATE), lambda bi, g: (0, g_off_b + g)),
            pl.BlockSpec((1, SSD_STATE), lambda bi, g: (0, g_off_c + g)),
            pl.BlockSpec((1, 1, 2 * HEADS_PER_GROUP), lambda bi, g: (g, 0, 0)),
            pl.BlockSpec((1, 1, 2 * HEADS_PER_GROUP), lambda bi, g: (g, 0, 0)),
            pl.BlockSpec((1, 2 * HEADS_PER_GROUP, 1), lambda bi, g: (g, 0, 0)),
            pl.BlockSpec((1, 2 * HEADS_PER_GROUP, 1), lambda bi, g: (g, 0, 0)),
            pl.BlockSpec((1, GROUP_COLS), lambda bi, g: (0, g)),
            pl.BlockSpec((1, GROUP_COLS), lambda bi, g: (0, g)),
        ],
        out_specs=pl.BlockSpec((1, s, GROUP_COLS), lambda bi, g: (bi, 0, g)),
        out_shape=jax.ShapeDtypeStruct((b, s, SSD_INNER), BF16),
        scratch_shapes=[
            pltpu.VMEM((s, GROUP_COLS), F32), pltpu.VMEM((s, SSD_STATE), F32), pltpu.VMEM((s, SSD_STATE), F32),
            pltpu.VMEM((s, GROUP_COLS), F32), pltpu.VMEM((SSD_STATE, GROUP_COLS), F32),
        ],
        compiler_params=_cparams(("parallel", "parallel")),
        name="ssd",
    )(xbc3, xbc3, xbc3, z3, dt4, dtt4, conv_w, conv_w, conv_w, conv_b, conv_b, conv_b, dtb, alog,
      dtb.reshape(SSD_GROUPS, -1, 1), alog.reshape(SSD_GROUPS, -1, 1), dsk, gn)


def _out_proj_kernel(x_ref, attn_ref, ssd_ref, wa_ref, ws_ref, g_ref, wr_ref, x1_ref, hn_ref, aff_ref):
    x1 = x_ref[...] + jnp.dot(attn_ref[...], wa_ref[...], preferred_element_type=F32) \
        + jnp.dot(ssd_ref[...], ws_ref[...], preferred_element_type=F32)
    x1_ref[...] = x1
    hn = _rms(x1, g_ref[...])
    hn_ref[...] = hn.astype(BF16)
    nt_dot = lambda a, b: lax.dot_general(a, b, (((1,), (1,)), ((), ())), preferred_element_type=F32)
    h_hi = hn.astype(BF16)
    h_lo = (hn - h_hi.astype(F32)).astype(BF16)
    wr = wr_ref[...]
    w_hi = wr.astype(BF16)
    w_lo = (wr - w_hi.astype(F32)).astype(BF16)
    logits = nt_dot(w_hi, h_hi) + (nt_dot(w_lo, h_hi) + nt_dot(w_hi, h_lo))
    e = jnp.exp(logits - jnp.max(logits, axis=0, keepdims=True))
    aff_ref[...] = e / jnp.sum(e, axis=0, keepdims=True)


def _out_proj(x2, attn2, ssd2, wa, ws, g_ffn, wr_t):
    t = x2.shape[0]
    tm = TM_PROJ
    full = lambda a: pl.BlockSpec(a.shape, lambda i: (0,) * a.ndim)
    return pl.pallas_call(
        _out_proj_kernel,
        grid=(t // tm,),
        in_specs=[pl.BlockSpec((tm, D_MODEL), lambda i: (i, 0)),
                  pl.BlockSpec((tm, ATTN_WIDTH), lambda i: (i, 0)),
                  pl.BlockSpec((tm, SSD_INNER), lambda i: (i, 0)),
                  full(wa), full(ws), full(g_ffn), full(wr_t)],
        out_specs=[pl.BlockSpec((tm, D_MODEL), lambda i: (i, 0)),
                   pl.BlockSpec((tm, D_MODEL), lambda i: (i, 0)),
                   pl.BlockSpec((N_EXPERTS, tm), lambda i: (0, i))],
        out_shape=[jax.ShapeDtypeStruct((t, D_MODEL), F32),
                   jax.ShapeDtypeStruct((t, D_MODEL), BF16),
                   jax.ShapeDtypeStruct((N_EXPERTS, t), F32)],
        compiler_params=_cparams(("parallel",)),
        name="out_proj",
    )(x2, attn2, ssd2, wa, ws, g_ffn, wr_t)


def _select_kernel(aff_ref, mask_ref, gate_ref, bits_ref, *, t, cap):
    nchunk = t // SEL_CHUNK
    lane = lax.broadcasted_iota(I32, (N_EXPERTS, SEL_CHUNK), 1)

    def to_bits(i, carry):
        sl = pl.ds(pl.multiple_of(i * SEL_CHUNK, SEL_CHUNK), SEL_CHUNK)
        bits_ref[:, sl] = lax.bitcast_convert_type(aff_ref[:, sl], I32)
        return carry
    lax.fori_loop(0, nchunk, to_bits, 0)

    def count(pred):
        def body(i, acc):
            sl = pl.ds(pl.multiple_of(i * SEL_CHUNK, SEL_CHUNK), SEL_CHUNK)
            return acc + pred(bits_ref[:, sl], lane + i * SEL_CHUNK).astype(F32)
        acc = lax.fori_loop(0, nchunk, body, jnp.zeros((N_EXPERTS, SEL_CHUNK), F32))
        return jnp.sum(acc, axis=1, keepdims=True)

    thr = jnp.zeros((N_EXPERTS, 1), I32)
    for bit in range(30, -1, -1):
        cand = thr | (1 << bit)
        cnt = count(lambda b, idx, cand=cand: b >= cand)
        thr = jnp.where(cnt >= cap, cand, thr)
    need = cap - count(lambda b, idx: b > thr)
    bound = jnp.zeros((N_EXPERTS, 1), I32)
    for bit in range(t.bit_length() - 1, -1, -1):
        cand = bound | (1 << bit)
        cnt = count(lambda b, idx, cand=cand: (b == thr) & (idx < cand))
        bound = jnp.where(cnt <= need, cand, bound)

    def emit(i, carry):
        sl = pl.ds(pl.multiple_of(i * SEL_CHUNK, SEL_CHUNK), SEL_CHUNK)
        b = bits_ref[:, sl]
        sel = (b > thr) | ((b == thr) & ((lane + i * SEL_CHUNK) < bound))
        mask_ref[:, sl] = sel.astype(F32)
        gate_ref[:, sl] = jnp.where(sel, aff_ref[:, sl], 0.0)
        return carry
    lax.fori_loop(0, nchunk, emit, 0)


def _select(aff_t, cap):
    t = aff_t.shape[1]
    full = pl.BlockSpec((N_EXPERTS, t), lambda i: (0, 0))
    return pl.pallas_call(
        functools.partial(_select_kernel, t=t, cap=cap),
        grid=(1,),
        in_specs=[full],
        out_specs=[full, full],
        out_shape=[jax.ShapeDtypeStruct((N_EXPERTS, t), F32)] * 2,
        scratch_shapes=[pltpu.VMEM((N_EXPERTS, t), I32)],
        compiler_params=_cparams(("arbitrary",)),
        name="select",
    )(aff_t)


def _dispatch_kernel(base_ref, cnt_ref, hn_ref, mask_ref, xs_ref, pos_ref,
                     rank_ref, onehot_ref, buf_ref, xbuf_ref, tail_ref, sem_ref, xsem_ref):
    i = pl.program_id(0)
    last = pl.num_programs(0) - 1
    tb = hn_ref.shape[0]
    slot = i % 2

    @pl.when(i == 0)
    def _():
        tail_ref[...] = jnp.zeros_like(tail_ref)
        xbuf_ref[...] = jnp.zeros_like(xbuf_ref)
        pad = [pltpu.make_async_copy(xbuf_ref, xs_ref.at[e, pl.ds(xs_ref.shape[1] - STRIP, STRIP), :], xsem_ref.at[0])
               for e in range(N_EXPERTS)]
        for cp in pad:
            cp.start()
        for cp in pad:
            cp.wait()

    r_i = lax.broadcasted_iota(I32, (tb, tb), 0)
    c_i = lax.broadcasted_iota(I32, (tb, tb), 1)
    incl = jnp.dot(mask_ref[...].astype(BF16), (r_i <= c_i).astype(BF16), preferred_element_type=F32)
    rank_ref[...] = incl - mask_ref[...]
    strip_row = lax.broadcasted_iota(I32, (STRIP, tb), 0).astype(F32)

    def onehot(e, k):
        off = (base_ref[i * N_EXPERTS + e] % BF16_ROWS).astype(F32)
        target = rank_ref[pl.ds(e, 1), :] + off
        sel = mask_ref[pl.ds(e, 1), :] > 0.0
        return (((strip_row + k * STRIP) == target) & sel).astype(BF16)

    def build(e, carry):
        pos_ref[pl.ds(e, 1), :] = rank_ref[pl.ds(e, 1), :] + base_ref[i * N_EXPERTS + e].astype(F32)
        onehot_ref[pl.ds(pl.multiple_of(e * STRIP, STRIP), STRIP), :] = onehot(e, 0)
        return carry
    for e in range(N_EXPERTS):
        build(e, 0)
    buf_ref[slot] = jnp.dot(onehot_ref[...], hn_ref[...], preferred_element_type=F32).astype(BF16)

    def strip_copy(s_, e, start):
        return pltpu.make_async_copy(buf_ref.at[s_, pl.ds(e * STRIP, STRIP), :],
                                     xs_ref.at[e, pl.ds(start, STRIP), :], sem_ref.at[s_, e])

    @pl.when(i > 0)
    def _():
        for e in range(N_EXPERTS):
            strip_copy(1 - slot, e, 0).wait()

    def finish(e, carry):
        base = base_ref[i * N_EXPERTS + e]
        off = base % BF16_ROWS
        end = off + cnt_ref[i * N_EXPERTS + e]
        row0 = pl.multiple_of(e * STRIP, STRIP)
        head = pl.ds(row0, BF16_ROWS)
        buf_ref[slot, head, :] = buf_ref[slot, head, :] + tail_ref[e]
        group = end // BF16_ROWS * BF16_ROWS
        has_tail = end % BF16_ROWS != 0

        @pl.when(jnp.logical_not(has_tail))
        def _():
            tail_ref[e] = jnp.zeros((BF16_ROWS, D_MODEL), BF16)

        @pl.when(has_tail & (group < STRIP))
        def _():
            tail_ref[e] = buf_ref[slot, pl.ds(pl.multiple_of(row0 + group, BF16_ROWS), BF16_ROWS), :]

        def extra(k, carry):
            xbuf_ref[...] = jnp.dot(onehot(e, k), hn_ref[...], preferred_element_type=F32).astype(BF16)

            @pl.when(has_tail & (group // STRIP == k))
            def _():
                tail_ref[e] = xbuf_ref[pl.ds(pl.multiple_of(group - k * STRIP, BF16_ROWS), BF16_ROWS), :]
            start = pl.multiple_of(base - off + k * STRIP, BF16_ROWS)
            cp = pltpu.make_async_copy(xbuf_ref, xs_ref.at[e, pl.ds(start, STRIP), :], xsem_ref.at[0])
            cp.start()
            cp.wait()
            return carry
        lax.fori_loop(1, jnp.maximum((end + STRIP - 1) // STRIP, 1), extra, 0)
        return carry
    for e in range(N_EXPERTS):
        finish(e, 0)

    for e in range(N_EXPERTS):
        base = base_ref[i * N_EXPERTS + e]
        strip_copy(slot, e, pl.multiple_of(base - base % BF16_ROWS, BF16_ROWS)).start()

    @pl.when(i == last)
    def _():
        for e in range(N_EXPERTS):
            strip_copy(slot, e, 0).wait()


def _dispatch(hn, mask_t, base, cnt, cap):
    t = hn.shape[0]
    tb = TB_ROUTE
    grid_spec = pltpu.PrefetchScalarGridSpec(
        num_scalar_prefetch=2,
        grid=(t // tb,),
        in_specs=[pl.BlockSpec((tb, D_MODEL), lambda i, *_: (i, 0)),
                  pl.BlockSpec((N_EXPERTS, tb), lambda i, *_: (0, i))],
        out_specs=[pl.BlockSpec(memory_space=pl.ANY),
                   pl.BlockSpec((N_EXPERTS, tb), lambda i, *_: (0, i))],
        scratch_shapes=[pltpu.VMEM((N_EXPERTS, tb), F32),
                        pltpu.VMEM((N_EXPERTS * STRIP, tb), BF16),
                        pltpu.VMEM((2, N_EXPERTS * STRIP, D_MODEL), BF16),
                        pltpu.VMEM((STRIP, D_MODEL), BF16),
                        pltpu.VMEM((N_EXPERTS, BF16_ROWS, D_MODEL), BF16),
                        pltpu.SemaphoreType.DMA((2, N_EXPERTS)), pltpu.SemaphoreType.DMA((1,))],
    )
    return pl.pallas_call(
        _dispatch_kernel,
        grid_spec=grid_spec,
        out_shape=[jax.ShapeDtypeStruct((N_EXPERTS, cap + STRIP, D_MODEL), BF16),
                   jax.ShapeDtypeStruct((N_EXPERTS, t), F32)],
        compiler_params=_cparams(("arbitrary",)),
        name="dispatch",
    )(base, cnt, hn, mask_t)


def _ffn_kernel(xs_ref, wg_ref, wu_ref, wd_ref, ye_ref):
    pad_tile = pl.program_id(1) == pl.num_programs(1) - 1

    @pl.when(jnp.logical_not(pad_tile))
    def _():
        x = xs_ref[0]
        g = jnp.dot(x, wg_ref[0], preferred_element_type=F32)
        u = jnp.dot(x, wu_ref[0], preferred_element_type=F32)
        a = (_silu(g) * u).astype(BF16)
        ye_ref[0] = jnp.dot(a, wd_ref[0], preferred_element_type=F32).astype(ye_ref.dtype)

    @pl.when(pad_tile)
    def _():
        ye_ref[...] = jnp.zeros_like(ye_ref)


def _ffn(xs, wg, wu, wd, cap):
    ntile = cap // TM_FFN
    once = pl.Buffered(1)
    return pl.pallas_call(
        _ffn_kernel,
        grid=(N_EXPERTS, ntile + 1),
        in_specs=[pl.BlockSpec((1, TM_FFN, D_MODEL), lambda e, j: (e, jnp.minimum(j, ntile - 1), 0)),
                  pl.BlockSpec((1, D_MODEL, EXPERT_FF), lambda e, j: (e, 0, 0), pipeline_mode=once),
                  pl.BlockSpec((1, D_MODEL, EXPERT_FF), lambda e, j: (e, 0, 0), pipeline_mode=once),
                  pl.BlockSpec((1, EXPERT_FF, D_MODEL), lambda e, j: (e, 0, 0), pipeline_mode=once)],
        out_specs=pl.BlockSpec((1, TM_FFN, D_MODEL), lambda e, j: (e, j, 0)),
        out_shape=jax.ShapeDtypeStruct((N_EXPERTS, cap + TM_FFN, D_MODEL), BF16),
        compiler_params=_cparams(("parallel", "parallel")),
        name="expert_ffn",
    )(xs, wg, wu, wd)


def _combine_kernel(start_ref, nstrip_ref, x1_ref, pos_ref, gate_ref, pe_ref, ye_ref, wple_ref, wpg_ref,
                    gple_ref, gpg_ref, gfin_ref, o_ref, acc_ref, w_ref, buf_ref, sem_ref, xbuf_ref, xsem_ref):
    i = pl.program_id(0)
    ntile = pl.num_programs(0) - 1
    tile = jnp.minimum(i, ntile - 1)
    tb = x1_ref.shape[0]
    slot = tile % 2
    lane = lax.broadcasted_iota(I32, (tb, LANES), 1)
    left = lane < STRIP
    lane_f = lane.astype(F32)

    def strip_copy(s_, t_, e):
        start = pl.multiple_of(start_ref[t_ * N_EXPERTS + e], BF16_ROWS)
        return pltpu.make_async_copy(ye_ref.at[e, pl.ds(start, STRIP), :],
                                     buf_ref.at[s_, pl.ds(e * STRIP, STRIP), :], sem_ref.at[s_, e])

    @pl.when(i == 0)
    def _():
        acc_ref[1] = jnp.zeros((tb, D_MODEL), F32)
        for e in range(N_EXPERTS):
            strip_copy(0, 0, e).start()

    @pl.when(i + 1 < ntile)
    def _():
        for e in range(N_EXPERTS):
            strip_copy(1 - slot, i + 1, e).start()

    @pl.when(i < ntile)
    def _():
        for e in range(N_EXPERTS):
            strip_copy(slot, i, e).wait()

    def rel_gate(e, width=LANES):
        rel = pos_ref[:, e:e + 1] - start_ref[tile * N_EXPERTS + e].astype(F32)
        return jnp.broadcast_to(rel, (tb, width)), jnp.broadcast_to(gate_ref[:, e:e + 1], (tb, width))

    x2 = x1_ref[...] + acc_ref[(i + 1) % 2]
    ple = _rms(jnp.dot(pe_ref[...].astype(BF16), wple_ref[...], preferred_element_type=F32), gple_ref[...])
    pg = _sigmoid(jnp.dot(_rms(x2, gpg_ref[...]).astype(BF16), wpg_ref[...], preferred_element_type=F32))
    o_ref[...] = _rms(x2 + pg * ple, gfin_ref[...])

    assert 2 * STRIP == LANES
    for pair in range(N_EXPERTS // 2):
        rel0, g0 = rel_gate(2 * pair)
        rel1, g1 = rel_gate(2 * pair + 1)
        target = jnp.where(left, rel0, rel1 + STRIP)
        gate = jnp.where(left, g0, g1)
        w_ref[:, pair * LANES:(pair + 1) * LANES] = jnp.where(lane_f == target, gate, 0.0).astype(BF16)
    acc_ref[i % 2] = jnp.dot(w_ref[...], buf_ref[slot], preferred_element_type=F32)

    more = functools.reduce(jnp.logical_or, [nstrip_ref[tile * N_EXPERTS + e] > 1 for e in range(N_EXPERTS)])

    @pl.when(more & (i < ntile))
    def _():
        for e in range(N_EXPERTS):
            def extra(k, carry, e=e):
                start = pl.multiple_of(start_ref[tile * N_EXPERTS + e] + k * STRIP, BF16_ROWS)
                cp = pltpu.make_async_copy(ye_ref.at[e, pl.ds(start, STRIP), :], xbuf_ref, xsem_ref.at[0])
                cp.start()
                rel, gate = rel_gate(e, STRIP)
                strip_lane = lax.broadcasted_iota(I32, (tb, STRIP), 1) + k * STRIP
                w = jnp.where(strip_lane.astype(F32) == rel, gate, 0.0).astype(BF16)
                cp.wait()
                acc_ref[i % 2] += jnp.dot(w, xbuf_ref[...], preferred_element_type=F32)
                return carry
            lax.fori_loop(1, jnp.maximum(nstrip_ref[tile * N_EXPERTS + e], 1), extra, 0)


def _combine(start, nstrip, x1, pos, gate, pe2, ye, wple, wpg, g_ple, g_pg, g_final):
    t = x1.shape[0]
    tb = TB_ROUTE
    ntile = t // tb
    full = lambda a: pl.BlockSpec(a.shape, lambda i, *_: (0,) * a.ndim)
    this = lambda i, *_: (jnp.minimum(i, ntile - 1), 0)
    prev = lambda i, *_: (jnp.maximum(i - 1, 0), 0)
    grid_spec = pltpu.PrefetchScalarGridSpec(
        num_scalar_prefetch=2,
        grid=(ntile + 1,),
        in_specs=[pl.BlockSpec((tb, D_MODEL), prev),
                  pl.BlockSpec((tb, N_EXPERTS), this),
                  pl.BlockSpec((tb, N_EXPERTS), this),
                  pl.BlockSpec((tb, PLE_DIM), prev),
                  pl.BlockSpec(memory_space=pl.ANY),
                  full(wple), full(wpg), full(g_ple), full(g_pg), full(g_final)],
        out_specs=pl.BlockSpec((tb, D_MODEL), prev),
        scratch_shapes=[pltpu.VMEM((2, tb, D_MODEL), F32),
                        pltpu.VMEM((tb, N_EXPERTS * STRIP), BF16),
                        pltpu.VMEM((2, N_EXPERTS * STRIP, D_MODEL), BF16), pltpu.SemaphoreType.DMA((2, N_EXPERTS)),
                        pltpu.VMEM((STRIP, D_MODEL), BF16), pltpu.SemaphoreType.DMA((1,))],
    )
    return pl.pallas_call(
        _combine_kernel,
        grid_spec=grid_spec,
        out_shape=jax.ShapeDtypeStruct((t, D_MODEL), F32),
        compiler_params=_cparams(("arbitrary",)),
        name="combine",
    )(start, nstrip, x1, pos, gate, pe2, ye, wple, wpg, g_ple, g_pg, g_final)


def _moe(x1, hn, aff_t, pe2, w):
    t = x1.shape[0]
    cap = max(1, CAPACITY_FACTOR * t // N_EXPERTS)
    assert cap % TM_FFN == 0 and t % TB_ROUTE == 0
    mask_t, gate_t = _select(aff_t, cap)
    nt = t // TB_ROUTE
    cnt = jnp.sum(mask_t.reshape(N_EXPERTS, nt, TB_ROUTE), axis=2).astype(I32).T
    base = jnp.cumsum(cnt, axis=0) - cnt
    xs, pos_t = _dispatch(hn, mask_t, base.reshape(-1), cnt.reshape(-1), cap)
    ye = _ffn(xs, w["wg"], w["wu"], w["wd"], cap)
    start = base // BF16_ROWS * BF16_ROWS
    nstrip = jnp.where(cnt > 0, (base - start + cnt + STRIP - 1) // STRIP, 0)
    return _combine(start.reshape(-1), nstrip.reshape(-1), x1, pos_t.T, gate_t.T, pe2, ye,
                    w["wple"], w["wpg"], w["g_ple"], w["g_pg"], w["g_final"])


def _trunk(x, pe, w):
    b, s, _ = x.shape
    t = b * s
    x2 = x.reshape(t, D_MODEL)
    qkv, z, xbc, dt, dtt = _in_proj(x2, w["g_mix"], w["wqkv"], w["wz"], w["wxbc"], w["wdt"])
    attn = _attention(qkv.reshape(b, s, 3 * ATTN_WIDTH), w["bias_tabs"])
    ssd = _ssd(xbc.reshape(b, s, CONV_DIM), z.reshape(b, s, SSD_INNER),
               dt.reshape(SSD_GROUPS, b, s, 2 * HEADS_PER_GROUP), dtt,
               w["conv_w"], w["conv_b"], w["dtb"], w["alog"], w["dsk"], w["gn"])
    x1, hn, aff_t = _out_proj(x2, attn.reshape(t, ATTN_WIDTH), ssd.reshape(t, SSD_INNER),
                              w["wo_a"], w["wo_s"], w["g_ffn"], w["wr_t"])
    y = _moe(x1, hn, aff_t, pe.reshape(t, PLE_DIM), w)
    return y.reshape(b, s, D_MODEL)


def _group_heads(a):
    parts = [jnp.concatenate([a[..., 0, g * HEADS_PER_GROUP:(g + 1) * HEADS_PER_GROUP],
                              a[..., 1, g * HEADS_PER_GROUP:(g + 1) * HEADS_PER_GROUP]], axis=-1)
             for g in range(SSD_GROUPS)]
    return jnp.stack(parts)


def _prepare(rel_bias, g_mix, w_in, conv_w, conv_b, dt_bias, a_log, d_skip, g_ssd, w_out, g_ffn, w_router,
             w_gate, w_up, w_down, g_pg, w_pg, w_ple, g_ple, g_final):
    row = lambda v: v.reshape(1, -1).astype(F32)
    c0 = 3 * ATTN_WIDTH
    c1 = c0 + SSD_INNER
    c2 = c1 + CONV_DIM
    w_dt = w_in[:, c2:].reshape(D_MODEL, 2, SSD_HEADS)
    return dict(
        g_mix=row(g_mix), wqkv=w_in[:, :c0].astype(BF16), wz=w_in[:, c0:c1].astype(BF16),
        wxbc=w_in[:, c1:c2].astype(BF16), wdt=_group_heads(w_dt).astype(BF16),
        bias_tabs=_attn_bias_tables(rel_bias),
        conv_w=conv_w.astype(F32), conv_b=row(conv_b),
        dtb=_group_heads(dt_bias.astype(F32)).reshape(SSD_GROUPS, 1, 2 * HEADS_PER_GROUP),
        alog=_group_heads(a_log.astype(F32)).reshape(SSD_GROUPS, 1, 2 * HEADS_PER_GROUP),
        dsk=row(jnp.repeat(d_skip, SSD_HEAD_DIM)), gn=row(g_ssd),
        wo_a=w_out[:ATTN_WIDTH].astype(BF16), wo_s=w_out[ATTN_WIDTH:].astype(BF16),
        g_ffn=row(g_ffn), wr_t=w_router.T.astype(F32),
        wg=w_gate.astype(BF16), wu=w_up.astype(BF16), wd=w_down.astype(BF16),
        g_pg=row(g_pg), wpg=w_pg.astype(BF16), wple=w_ple.astype(BF16), g_ple=row(g_ple), g_final=row(g_final),
    )


def kernel(x_prompt, x_sample, p_prompt, p_sample, rel_bias, g_mix, w_in, conv_w, conv_b, dt_bias, a_log, d_skip,
           g_ssd, w_out, g_ffn, w_router, w_gate, w_up, w_down, g_pg, w_pg, w_ple, g_ple, g_final):
    assert g_mix.shape[0] == 1, "single-layer trunk"
    w = _prepare(rel_bias, g_mix[0], w_in[0], conv_w[0], conv_b[0], dt_bias[0], a_log[0], d_skip[0], g_ssd[0],
                 w_out[0], g_ffn[0], w_router[0], w_gate[0], w_up[0], w_down[0], g_pg[0], w_pg[0], w_ple[0],
                 g_ple[0], g_final)
    return _trunk(x_prompt, p_prompt[0], w), _trunk(x_sample, p_sample[0], w)
```
